```python
import math
import jax, jax.numpy as jnp
from jax import lax
import numpy as np

D_MODEL = 1024
BATCH = 1
SEQ = 16384
DEPTH = 1
DEC_BATCH = 32
DEC_SEQ = 16
PAST_LEN = 1024

CHUNK = 64
N_META = 16
Q_BLOCK = 128
HEAD_DIM = 64
N_HEADS_DIFF = 4
N_HEADS_SB = 8
DIFF_WIDTH = N_HEADS_DIFF * 2 * HEAD_DIM
SB_WIDTH = N_HEADS_SB * HEAD_DIM
MIX_WIDTH = DIFF_WIDTH + SB_WIDTH
Q_COLS = DIFF_WIDTH + SB_WIDTH
D_IN = 3 * MIX_WIDTH
D_FF = 4 * D_MODEL
ROPE_THETA = 10000.0
NORM_EPS = 1e-6
NEG_INF = -1e30

kernel_name = "hymba_diff_stickbreak_streaming_step"


def rmsnorm(x, g):
    xf = x.astype(jnp.float32)
    y = xf * lax.rsqrt(jnp.mean(xf * xf, axis=-1, keepdims=True) + NORM_EPS)
    return (y * g.astype(jnp.float32)).astype(x.dtype)


def rope(x, pos):
    half = HEAD_DIM // 2
    inv_freq = ROPE_THETA ** (-jnp.arange(half, dtype=jnp.float32) / half)
    ang = pos.astype(jnp.float32)[:, None] * inv_freq[None, :]
    shape = (pos.shape[0],) + (1,) * (x.ndim - 3) + (half,)
    cos = jnp.cos(ang).reshape(shape)
    sin = jnp.sin(ang).reshape(shape)
    xf = x.astype(jnp.float32)
    x1, x2 = xf[..., :half], xf[..., half:]
    return jnp.concatenate([x1 * cos - x2 * sin, x2 * cos + x1 * sin], axis=-1).astype(x.dtype)


def chunk_id(pos):
    return jnp.where(pos < N_META, -1, (pos - N_META) // CHUNK)


def split_q(zq, pos):
    B, T, _ = zq.shape
    dq, sq = jnp.split(zq, [DIFF_WIDTH], axis=-1)
    dq = rope(dq.reshape(B, T, N_HEADS_DIFF, 2, HEAD_DIM), pos)
    sq = sq.reshape(B, T, N_HEADS_SB, HEAD_DIM)
    return dq, sq


def split_kv(zkv, pos):
    B, T, _ = zkv.shape
    dk, sk, dv, sv = jnp.split(zkv, [DIFF_WIDTH, MIX_WIDTH, MIX_WIDTH + DIFF_WIDTH], axis=-1)
    dk = rope(dk.reshape(B, T, N_HEADS_DIFF, 2, HEAD_DIM), pos)
    dv = dv.reshape(B, T, N_HEADS_DIFF, 2 * HEAD_DIM)
    sk = sk.reshape(B, T, N_HEADS_SB, HEAD_DIM)
    sv = sv.reshape(B, T, N_HEADS_SB, HEAD_DIM)
    return dk, dv, sk, sv


def diff_attn(q, k, v, mask, lam, g_head, lambda_init):
    B, Tq = q.shape[0], q.shape[1]
    s = jnp.einsum('bqhcd,bkhcd->bhcqk', q.astype(jnp.float32), k.astype(jnp.float32)) * (HEAD_DIM ** -0.5)
    p = jax.nn.softmax(jnp.where(mask, s, NEG_INF), axis=-1)
    w = p[:, :, 0] - lam * p[:, :, 1]
    o = jnp.einsum('bhqk,bkhe->bqhe', w, v.astype(jnp.float32))
    o = rmsnorm(o, g_head) * (1.0 - lambda_init)
    return o.reshape(B, Tq, DIFF_WIDTH)


def sb_attn(q, k, v, mask):
    B, Tq = q.shape[0], q.shape[1]
    z = jnp.einsum('bqhd,bkhd->bhqk', q.astype(jnp.float32), k.astype(jnp.float32)) * (HEAD_DIM ** -0.5)
    log_beta = jax.nn.log_sigmoid(z)
    log_keep = jnp.where(mask, jax.nn.log_sigmoid(-z), 0.0)
    rc = lax.cumsum(log_keep, axis=log_keep.ndim - 1, reverse=True)
    after = jnp.concatenate([rc[..., 1:], jnp.zeros_like(rc[..., :1])], axis=-1)
    a = jnp.where(mask, jnp.exp(log_beta + after), 0.0)
    o = jnp.einsum('bhqk,bkhd->bqhd', a, v.astype(jnp.float32))
    return o.reshape(B, Tq, SB_WIDTH)


def attend(dq, sq, dk, dv, sk, sv, pos_q, pos_k, lam, g_head, lambda_init):
    mask_chunk = chunk_id(pos_k)[None, :] <= chunk_id(pos_q)[:, None]
    mask_strict = pos_k[None, :] < pos_q[:, None]
    return jnp.concatenate([diff_attn(dq, dk, dv, mask_chunk, lam, g_head, lambda_init),
                            sb_attn(sq, sk, sv, mask_strict)], axis=-1)


def attend_blocked(dq, sq, dk, dv, sk, sv, pos, lam, g_head, lambda_init):
    B, T = dq.shape[0], dq.shape[1]
    n_blk = -(-T // Q_BLOCK)
    pad = n_blk * Q_BLOCK - T
    dq_p = jnp.pad(dq, ((0, 0), (0, pad), (0, 0), (0, 0), (0, 0)))
    sq_p = jnp.pad(sq, ((0, 0), (0, pad), (0, 0), (0, 0)))

    def blk(i):
        start = i * Q_BLOCK
        qd = lax.dynamic_slice_in_dim(dq_p, start, Q_BLOCK, axis=1)
        qs = lax.dynamic_slice_in_dim(sq_p, start, Q_BLOCK, axis=1)
        pos_q = start + jnp.arange(Q_BLOCK)
        return attend(qd, qs, dk, dv, sk, sv, pos_q, pos, lam, g_head, lambda_init)

    out = lax.map(blk, jnp.arange(n_blk))
    return jnp.moveaxis(out, 0, 1).reshape(B, n_blk * Q_BLOCK, MIX_WIDTH)[:, :T]


def layer(x, pos, past, past_pos, p, blocked):
    h = rmsnorm(x, p['g_mix'])
    z = h @ p['w_in']
    dq, sq = split_q(z[..., :Q_COLS], pos)
    dk, dv, sk, sv = split_kv(z[..., Q_COLS:], pos)
    lam = (jnp.exp(jnp.sum(p['lq1'].astype(jnp.float32) * p['lk1'].astype(jnp.float32)))
           - jnp.exp(jnp.sum(p['lq2'].astype(jnp.float32) * p['lk2'].astype(jnp.float32)))
           + p['lambda_init'])
    if past is None:
        kd, vd, ks, vs, pos_k = dk, dv, sk, sv, pos
    else:
        kd = jnp.concatenate([past[0].astype(dk.dtype), dk], axis=1)
        vd = jnp.concatenate([past[1].astype(dv.dtype), dv], axis=1)
        ks = jnp.concatenate([past[2].astype(sk.dtype), sk], axis=1)
        vs = jnp.concatenate([past[3].astype(sv.dtype), sv], axis=1)
        pos_k = jnp.concatenate([past_pos, pos])
    if blocked:
        mixed = attend_blocked(dq, sq, kd, vd, ks, vs, pos, lam, p['g_head'], p['lambda_init'])
    else:
        mixed = attend(dq, sq, kd, vd, ks, vs, pos, pos_k, lam, p['g_head'], p['lambda_init'])
    x = x + mixed.astype(x.dtype) @ p['w_out']
    h2 = rmsnorm(x, p['g_mlp'])
    x = x + jnp.square(jax.nn.relu(h2 @ p['w_up'])) @ p['w_down']
    return x, (dk, dv, sk, sv)


def setup_inputs(seed: int = 0) -> dict:
    key = jax.random.key(seed)
    ks = jax.random.split(key, 20)
    f32 = jnp.float32
    nrm = lambda k, s, sc: jax.random.normal(k, s, f32) * sc
    return {
        'x_prompt': nrm(ks[0], (BATCH, SEQ, D_MODEL), 1.0),
        'x_sample': nrm(ks[1], (DEC_BATCH, DEC_SEQ, D_MODEL), 1.0),
        'cache_diff_k': nrm(ks[2], (DEPTH, DEC_BATCH, PAST_LEN, N_HEADS_DIFF, 2, HEAD_DIM), 1.0),
        'cache_diff_v': nrm(ks[3], (DEPTH, DEC_BATCH, PAST_LEN, N_HEADS_DIFF, 2 * HEAD_DIM), 1.0),
        'cache_sb_k': nrm(ks[4], (DEPTH, DEC_BATCH, PAST_LEN, N_HEADS_SB, HEAD_DIM), 1.0),
        'cache_sb_v': nrm(ks[5], (DEPTH, DEC_BATCH, PAST_LEN, N_HEADS_SB, HEAD_DIM), 1.0),
        'meta_tokens': nrm(ks[6], (N_META, D_MODEL), 1.0),
        'g_mix': 1.0 + nrm(ks[7], (DEPTH, D_MODEL), 0.02),
        'w_in': nrm(ks[8], (DEPTH, D_MODEL, D_IN), D_MODEL ** -0.5),
        'lambda_q1': nrm(ks[9], (DEPTH, HEAD_DIM), 0.1),
        'lambda_k1': nrm(ks[10], (DEPTH, HEAD_DIM), 0.1),
        'lambda_q2': nrm(ks[11], (DEPTH, HEAD_DIM), 0.1),
        'lambda_k2': nrm(ks[12], (DEPTH, HEAD_DIM), 0.1),
        'g_diff_head': 1.0 + nrm(ks[13], (DEPTH, 2 * HEAD_DIM), 0.02),
        'w_out': nrm(ks[14], (DEPTH, MIX_WIDTH, D_MODEL), MIX_WIDTH ** -0.5),
        'g_mlp': 1.0 + nrm(ks[15], (DEPTH, D_MODEL), 0.02),
        'w_up': nrm(ks[16], (DEPTH, D_MODEL, D_FF), D_MODEL ** -0.5),
        'w_down': nrm(ks[17], (DEPTH, D_FF, D_MODEL), D_FF ** -0.5),
        'g_final': 1.0 + nrm(ks[18], (D_MODEL,), 0.02),
    }


def reference(x_prompt, x_sample, cache_diff_k, cache_diff_v, cache_sb_k, cache_sb_v,
              meta_tokens, g_mix, w_in, lambda_q1, lambda_k1, lambda_q2, lambda_k2,
              g_diff_head, w_out, g_mlp, w_up, w_down, g_final):
    Bp, Tp, D = x_prompt.shape
    Bs, Ts, _ = x_sample.shape
    P = cache_diff_k.shape[2]

    def params(l):
        return {'g_mix': g_mix[l], 'w_in': w_in[l], 'lq1': lambda_q1[l], 'lk1': lambda_k1[l],
                'lq2': lambda_q2[l], 'lk2': lambda_k2[l], 'g_head': g_diff_head[l],
                'w_out': w_out[l], 'g_mlp': g_mlp[l], 'w_up': w_up[l], 'w_down': w_down[l],
                'lambda_init': 0.8 - 0.6 * math.exp(-0.3 * l)}

    xp = jnp.concatenate([jnp.broadcast_to(meta_tokens[None].astype(x_prompt.dtype), (Bp, N_META, D)),
                          x_prompt], axis=1)
    pos_p = jnp.arange(N_META + Tp)
    pk_d, pv_d, pk_s, pv_s = [], [], [], []
    for l in range(DEPTH):
        xp, (dk, dv, sk, sv) = layer(xp, pos_p, None, None, params(l), True)
        pk_d.append(dk); pv_d.append(dv); pk_s.append(sk); pv_s.append(sv)
    y_prompt = rmsnorm(xp, g_final)[:, N_META:]

    pos_meta = jnp.arange(N_META)
    pos_cache = N_META + jnp.arange(P)
    pos_new = N_META + P + jnp.arange(Ts)
    past_pos = jnp.concatenate([pos_meta, pos_cache])
    h_meta = meta_tokens[None].astype(x_sample.dtype)
    xs = x_sample
    sk_d, sv_d, sk_s, sv_s = [], [], [], []
    for l in range(DEPTH):
        p = params(l)
        if l < DEPTH - 1:
            h_meta_next, meta_kv = layer(h_meta, pos_meta, None, None, p, False)
        else:
            h_meta_next = h_meta
            meta_kv = split_kv(rmsnorm(h_meta, p['g_mix']) @ p['w_in'][:, Q_COLS:], pos_meta)
        caches = (cache_diff_k[l], cache_diff_v[l], cache_sb_k[l], cache_sb_v[l])
        past = tuple(jnp.concatenate([jnp.broadcast_to(m.astype(c.dtype), (Bs,) + m.shape[1:]), c], axis=1)
                     for m, c in zip(meta_kv, caches))
        xs, (dk, dv, sk, sv) = layer(xs, pos_new, past, past_pos, p, False)
        sk_d.append(dk); sv_d.append(dv); sk_s.append(sk); sv_s.append(sv)
        h_meta = h_meta_next
    y_sample = rmsnorm(xs, g_final)

    return (y_prompt, y_sample,
            jnp.stack(pk_d), jnp.stack(pv_d), jnp.stack(pk_s), jnp.stack(pv_s),
            jnp.stack(sk_d), jnp.stack(sv_d), jnp.stack(sk_s), jnp.stack(sv_s))
```

```python
import functools
import math

import jax
import jax.numpy as jnp
from jax import lax
from jax.experimental import pallas as pl
from jax.experimental.pallas import tpu as pltpu

D_MODEL = 1024
CHUNK = 64
CHUNK_SHIFT = 6
N_META = 16
HEAD_DIM = 64
N_HEADS_DIFF = 4
N_HEADS_SB = 8
DIFF_WIDTH = N_HEADS_DIFF * 2 * HEAD_DIM
SB_WIDTH = N_HEADS_SB * HEAD_DIM
MIX_WIDTH = DIFF_WIDTH + SB_WIDTH
D_IN = 3 * MIX_WIDTH
D_FF = 4 * D_MODEL
ROPE_THETA = 10000.0
NORM_EPS = 1e-6
NEG_INF = -1e30
LAMBDA_INIT = 0.8 - 0.6 * math.exp(-0.3 * 0)

LANES = 128
META_PAD = 128
SB_LOG_STOP = -90.0

VMEM_LIMIT = 56 * 1024 * 1024

_f32 = jnp.float32
_bf16 = jnp.bfloat16


def _dot(a, b):
    return jnp.dot(a, b, preferred_element_type=_f32)


def _dot_nt(a, b):
    return lax.dot_general(a, b, (((1,), (1,)), ((), ())), preferred_element_type=_f32)


def _rms(x, g):
    return x * lax.rsqrt(jnp.mean(x * x, axis=-1, keepdims=True) + NORM_EPS) * g


def _log_sigmoid(s):
    return jnp.minimum(s, 0.0) - jnp.log1p(jnp.exp(-jnp.abs(s)))


def _proj_kernel(x_ref, g_ref, w_ref, cos_ref, sin_ref,
                 qd_ref, qs_ref, kd_ref, ks_ref, vd_ref, vs_ref,
                 kdb_ref, ksb_ref, vdb_ref, vsb_ref):
    h = _rms(x_ref[...], g_ref[...]).astype(_bf16)
    cos = cos_ref[...]
    sin = sin_ref[...]
    lane = lax.broadcasted_iota(jnp.int32, cos.shape, 1)
    first_half = (lane & (HEAD_DIM - 1)) < (HEAD_DIM // 2)

    def rope(z):
        parts = []
        for j in range(z.shape[1] // LANES):
            zj = z[:, j * LANES:(j + 1) * LANES]
            swapped = jnp.where(first_half,
                                pltpu.roll(zj, LANES - HEAD_DIM // 2, 1),
                                pltpu.roll(zj, HEAD_DIM // 2, 1))
            parts.append(zj * cos + swapped * sin)
        return jnp.concatenate(parts, axis=1)

    def cols(c):
        return _dot(h, w_ref[:, c * DIFF_WIDTH:(c + 1) * DIFF_WIDTH])

    scale = HEAD_DIM ** -0.5
    qd_ref[...] = (rope(cols(0)) * scale).astype(_bf16)
    qs_ref[...] = (cols(1) * scale).astype(_bf16)
    kd = rope(cols(2))
    kd_ref[...] = kd
    kdb_ref[...] = kd.astype(_bf16)
    ks = cols(3)
    ks_ref[...] = ks
    ksb_ref[...] = ks.astype(_bf16)
    vd = cols(4)
    vd_ref[...] = vd
    vdb_ref[...] = vd.astype(_bf16)
    vs = cols(5)
    vs_ref[...] = vs
    vsb_ref[...] = vs.astype(_bf16)


def _proj(x, g, w_bf16, cos, sin, tm):
    rows = x.shape[0]
    row_blk = lambda width: pl.BlockSpec((tm, width), lambda i: (i, 0))
    full = lambda a: pl.BlockSpec(a.shape, lambda i: (0, 0))
    f32_out = jax.ShapeDtypeStruct((rows, DIFF_WIDTH), _f32)
    b16_out = jax.ShapeDtypeStruct((rows, DIFF_WIDTH), _bf16)
    return pl.pallas_call(
        _proj_kernel,
        grid=(rows // tm,),
        in_specs=[row_blk(D_MODEL), full(g), full(w_bf16), row_blk(LANES), row_blk(LANES)],
        out_specs=[row_blk(DIFF_WIDTH)] * 10,
        out_shape=[b16_out, b16_out, f32_out, f32_out, f32_out, f32_out,
                   b16_out, b16_out, b16_out, b16_out],
        compiler_params=pltpu.CompilerParams(
            dimension_semantics=("arbitrary",), vmem_limit_bytes=VMEM_LIMIT),
        name="proj",
    )(x, g, w_bf16, cos, sin)


def _rope_tables(pos):
    half = HEAD_DIM // 2
    inv_freq = ROPE_THETA ** (-jnp.arange(half, dtype=_f32) / half)
    ang = pos.astype(_f32)[:, None] * inv_freq[None, :]
    cos = jnp.cos(ang)
    sin = jnp.sin(ang)
    reps = LANES // HEAD_DIM
    return (jnp.concatenate([cos, cos] * reps, axis=1),
            jnp.concatenate([-sin, sin] * reps, axis=1))


def _lane_half_masks(shape):
    lane = lax.broadcasted_iota(jnp.int32, shape, 1)
    return lane < HEAD_DIM, lane >= HEAD_DIM


def _lambda(lq1, lk1, lq2, lk2):
    return (jnp.exp(jnp.sum(lq1 * lk1, axis=-1, keepdims=True))
            - jnp.exp(jnp.sum(lq2 * lk2, axis=-1, keepdims=True)) + LAMBDA_INIT)


def _diff_finish(o0, l0, o1, l1, lam, g_head):
    o = o0 / l0 - lam * (o1 / l1)
    return _rms(o, g_head) * (1.0 - LAMBDA_INIT)


def _diff_kernel(q_ref, k_ref, v_ref, km_ref, vm_ref, lq1_ref, lk1_ref, lq2_ref, lk2_ref, gh_ref,
                 o_ref, vaug_ref, m_ref, acc_ref, *, bq, bk, n_keys):
    qi = pl.program_id(1)

    @pl.when(qi == 0)
    def _():
        ones = jnp.ones((n_keys + META_PAD, LANES), _bf16)
        vaug_ref[:, LANES:] = ones
        vaug_ref[:n_keys, :LANES] = v_ref[...]
        vaug_ref[n_keys:, :LANES] = vm_ref[...]

    q = q_ref[...]
    lo, hi = _lane_half_masks(q.shape)
    zero = jnp.zeros_like(q)
    qz = jnp.concatenate([jnp.where(lo, q, zero), jnp.where(hi, q, zero)], axis=0)

    m_ref[...] = jnp.full(m_ref.shape, NEG_INF, _f32)
    acc_ref[...] = jnp.zeros(acc_ref.shape, _f32)

    def step(k, vaug, mask):
        s = _dot_nt(qz, k)
        if mask is not None:
            s = jnp.where(mask, s, NEG_INF)
        m_prev = m_ref[...]
        m_new = jnp.maximum(m_prev, jnp.max(s, axis=-1, keepdims=True))
        p = jnp.exp(s - m_new).astype(_bf16)
        acc_ref[...] = jnp.exp(m_prev - m_new) * acc_ref[...] + _dot(p, vaug)
        m_ref[...] = m_new

    q0 = qi * bq
    n_full = (q0 + CHUNK) // bk

    def full_tile(j, carry):
        start = pl.multiple_of(j * bk, bk)
        step(k_ref[pl.ds(start, bk), :], vaug_ref[pl.ds(start, bk), :], None)
        return carry

    lax.fori_loop(0, n_full, full_tile, 0)

    start = pl.multiple_of(n_full * bk, bk)
    row = (lax.broadcasted_iota(jnp.int32, (2 * bq, bk), 0) & (bq - 1)) + q0
    col = lax.broadcasted_iota(jnp.int32, (2 * bq, bk), 1) + start
    step(k_ref[pl.ds(start, bk), :], vaug_ref[pl.ds(start, bk), :],
         (col >> CHUNK_SHIFT) <= (row >> CHUNK_SHIFT))

    mcol = lax.broadcasted_iota(jnp.int32, (2 * bq, META_PAD), 1)
    step(km_ref[...], vaug_ref[n_keys:, :], mcol < N_META)

    acc = acc_ref[...]
    lam = _lambda(lq1_ref[...], lk1_ref[...], lq2_ref[...], lk2_ref[...])
    out = _diff_finish(acc[:bq, :LANES], acc[:bq, LANES:LANES + 1],
                       acc[bq:, :LANES], acc[bq:, LANES:LANES + 1], lam, gh_ref[...])
    o_ref[...] = out.astype(_bf16)


def _diff_attn(qd, kdb, vdb, kmeta, vmeta, lq1, lk1, lq2, lk2, g_head, bq=256, bk=512):
    n = qd.shape[0]
    head_blk = lambda rows: pl.BlockSpec((rows, LANES), lambda h, i: (0, h))
    small = lambda a: pl.BlockSpec(a.shape, lambda h, i: (0, 0))
    return pl.pallas_call(
        functools.partial(_diff_kernel, bq=bq, bk=bk, n_keys=n),
        grid=(N_HEADS_DIFF, n // bq),
        in_specs=[pl.BlockSpec((bq, LANES), lambda h, i: (i, h)),
                  head_blk(n), head_blk(n), head_blk(META_PAD), head_blk(META_PAD),
                  small(lq1), small(lk1), small(lq2), small(lk2), small(g_head)],
        out_specs=pl.BlockSpec((bq, LANES), lambda h, i: (i, h)),
        out_shape=jax.ShapeDtypeStruct((n, DIFF_WIDTH), _bf16),
        scratch_shapes=[pltpu.VMEM((n + META_PAD, 2 * LANES), _bf16),
                        pltpu.VMEM((2 * bq, 1), _f32),
                        pltpu.VMEM((2 * bq, 2 * LANES), _f32)],
        compiler_params=pltpu.CompilerParams(
            dimension_semantics=("arbitrary", "arbitrary"), vmem_limit_bytes=VMEM_LIMIT),
        name="diff_attn",
    )(qd, kdb, vdb, kmeta, vmeta, lq1, lk1, lq2, lk2, g_head)


def _suffix_matrix(n):
    j = lax.broadcasted_iota(jnp.int32, (n, n), 0)
    s = lax.broadcasted_iota(jnp.int32, (n, n), 1)
    return jnp.where(j > s, 1.0, 0.0).astype(_bf16)


def _sb_tile(qz, k, v, carry, mask, suffix):
    s = _dot_nt(qz, k)
    log_beta = _log_sigmoid(s)
    log_keep = log_beta - s
    if mask is not None:
        log_keep = jnp.where(mask, log_keep, 0.0)
    keep_hi = log_keep.astype(_bf16)
    keep_lo = (log_keep - keep_hi.astype(_f32)).astype(_bf16)
    after = _dot(keep_hi, suffix) + _dot(keep_lo, suffix)
    a = jnp.exp(log_beta + after + carry)
    if mask is not None:
        a = jnp.where(mask, a, 0.0)
    total = after[:, 0:1] + log_keep[:, 0:1]
    return _dot(a.astype(_bf16), v), carry + total


def _sb_kernel(q_ref, k_ref, v_ref, km_ref, vm_ref, o_ref, acc_ref, carry_ref, *, bq):
    qi = pl.program_id(1)
    q = q_ref[...]
    lo, hi = _lane_half_masks(q.shape)
    zero = jnp.zeros_like(q)
    suffix = _suffix_matrix(bq)
    q0 = qi * bq
    row = lax.broadcasted_iota(jnp.int32, (bq, bq), 0)
    col = lax.broadcasted_iota(jnp.int32, (bq, bq), 1)
    mcol = lax.broadcasted_iota(jnp.int32, (bq, META_PAD), 1)

    outs = []
    for head_mask in (lo, hi):
        qz = jnp.where(head_mask, q, zero)
        start = pl.multiple_of(q0, bq)
        pv, carry = _sb_tile(qz, k_ref[pl.ds(start, bq), :], v_ref[pl.ds(start, bq), :],
                             jnp.zeros((bq, 1), _f32), col < row, suffix)
        acc_ref[...] = pv
        carry_ref[...] = carry

        def cond(state):
            j, live = state
            return jnp.logical_and(j >= 0, live > SB_LOG_STOP)

        def body(state):
            j, _ = state
            start = pl.multiple_of(j * bq, bq)
            pv, carry = _sb_tile(qz, k_ref[pl.ds(start, bq), :], v_ref[pl.ds(start, bq), :],
                                 carry_ref[...], None, suffix)
            acc_ref[...] += pv
            carry_ref[...] = carry
            return j - 1, jnp.max(carry)

        _, live = lax.while_loop(cond, body, (qi - 1, jnp.max(carry)))

        @pl.when(live > SB_LOG_STOP)
        def _():
            pv, _ = _sb_tile(qz, km_ref[...], vm_ref[...], carry_ref[...], mcol < N_META,
                             _suffix_matrix(META_PAD))
            acc_ref[...] += pv

        outs.append(acc_ref[...])
    o_ref[...] = jnp.where(lo, outs[0], outs[1]).astype(_bf16)


def _sb_attn(qs, ksb, vsb, kmeta, vmeta, bq=128):
    n = qs.shape[0]
    pair_blk = lambda rows: pl.BlockSpec((rows, LANES), lambda h, i: (0, h))
    return pl.pallas_call(
        functools.partial(_sb_kernel, bq=bq),
        grid=(SB_WIDTH // LANES, n // bq),
        in_specs=[pl.BlockSpec((bq, LANES), lambda h, i: (i, h)),
                  pair_blk(n), pair_blk(n), pair_blk(META_PAD), pair_blk(META_PAD)],
        out_specs=pl.BlockSpec((bq, LANES), lambda h, i: (i, h)),
        out_shape=jax.ShapeDtypeStruct((n, SB_WIDTH), _bf16),
        scratch_shapes=[pltpu.VMEM((bq, LANES), _f32), pltpu.VMEM((bq, 1), _f32)],
        compiler_params=pltpu.CompilerParams(
            dimension_semantics=("arbitrary", "arbitrary"), vmem_limit_bytes=VMEM_LIMIT),
        name="sb_attn",
    )(qs, ksb, vsb, kmeta, vmeta)


def _samp_kernel(qd_ref, qs_ref, kdn_ref, ksn_ref, vdn_ref, vsn_ref,
                 kdm_ref, ksm_ref, vdm_ref, vsm_ref,
                 ckd_ref, cks_ref, cvd_ref, cvs_ref,
                 lq1_ref, lk1_ref, lq2_ref, lk2_ref, gh_ref,
                 od_ref, os_ref, kd_cat, ks_cat, vd_cat, vs_cat, *, ts, past, n_cat):
    new0 = N_META + past
    for cat, meta, cache, new in ((kd_cat, kdm_ref, ckd_ref, kdn_ref), (ks_cat, ksm_ref, cks_ref, ksn_ref),
                                  (vd_cat, vdm_ref, cvd_ref, vdn_ref), (vs_cat, vsm_ref, cvs_ref, vsn_ref)):
        cat[0:N_META, :] = meta[0:N_META, :]
        cat[N_META:new0, :] = cache[0].astype(_bf16)
        cat[new0:new0 + ts, :] = new[...]
        cat[new0 + ts:, :] = jnp.zeros((n_cat - new0 - ts, DIFF_WIDTH), _bf16)

    pos_k = lax.broadcasted_iota(jnp.int32, (ts, n_cat), 1)
    pos_q = lax.broadcasted_iota(jnp.int32, (ts, n_cat), 0) + new0
    chunk = lambda p: jnp.where(p < N_META, -1, (p - N_META) >> CHUNK_SHIFT)
    mask_chunk = jnp.logical_and(pos_k < new0 + ts, chunk(pos_k) <= chunk(pos_q))
    tile_k = lax.broadcasted_iota(jnp.int32, (ts, LANES), 1)
    tile_q = lax.broadcasted_iota(jnp.int32, (ts, LANES), 0) + new0

    lam = _lambda(lq1_ref[...], lk1_ref[...], lq2_ref[...], lk2_ref[...])
    lo, hi = _lane_half_masks((ts, LANES))
    zero = jnp.zeros((ts, LANES), _bf16)

    for h in range(N_HEADS_DIFF):
        sl = slice(h * LANES, (h + 1) * LANES)
        q = qd_ref[:, sl]
        k = kd_cat[:, sl]
        v = vd_cat[:, sl]
        parts = []
        for half in (lo, hi):
            s = jnp.where(mask_chunk, _dot_nt(jnp.where(half, q, zero), k), NEG_INF)
            p = jnp.exp(s - jnp.max(s, axis=-1, keepdims=True))
            parts.append((_dot(p.astype(_bf16), v), jnp.sum(p, axis=-1, keepdims=True)))
        out = _diff_finish(parts[0][0], parts[0][1], parts[1][0], parts[1][1], lam, gh_ref[...])
        od_ref[:, sl] = out.astype(_bf16)

    suffix = _suffix_matrix(LANES)
    n_tiles = n_cat // LANES
    for hp in range(SB_WIDTH // LANES):
        sl = slice(hp * LANES, (hp + 1) * LANES)
        q = qs_ref[:, sl]
        outs = []
        for half in (lo, hi):
            qz = jnp.where(half, q, zero)
            carry = jnp.zeros((ts, 1), _f32)
            acc = jnp.zeros((ts, LANES), _f32)
            for t in reversed(range(n_tiles)):
                rows = slice(t * LANES, (t + 1) * LANES)
                pv, carry = _sb_tile(qz, ks_cat[rows, sl], vs_cat[rows, sl], carry,
                                     tile_k + t * LANES < tile_q, suffix)
                acc = acc + pv
            outs.append(acc)
        os_ref[:, sl] = jnp.where(lo, outs[0], outs[1]).astype(_bf16)


def _samp_attn(qd, qs, kdn, ksn, vdn, vsn, kdm, ksm, vdm, vsm, ckd, cks, cvd, cvs,
               lq1, lk1, lq2, lk2, g_head, ts):
    nb, past, _ = ckd.shape
    n_cat = -(-(N_META + past + ts) // LANES) * LANES
    row_blk = pl.BlockSpec((ts, DIFF_WIDTH), lambda b: (b, 0))
    cache_blk = pl.BlockSpec((1, past, DIFF_WIDTH), lambda b: (b, 0, 0))
    small = lambda a: pl.BlockSpec(a.shape, lambda b: (0, 0))
    out = jax.ShapeDtypeStruct((nb * ts, DIFF_WIDTH), _bf16)
    return pl.pallas_call(
        functools.partial(_samp_kernel, ts=ts, past=past, n_cat=n_cat),
        grid=(nb,),
        in_specs=[row_blk] * 6 + [small(kdm)] * 4 + [cache_blk] * 4
                 + [small(lq1), small(lk1), small(lq2), small(lk2), small(g_head)],
        out_specs=[row_blk, row_blk],
        out_shape=[out, out],
        scratch_shapes=[pltpu.VMEM((n_cat, DIFF_WIDTH), _bf16)] * 4,
        compiler_params=pltpu.CompilerParams(
            dimension_semantics=("arbitrary",), vmem_limit_bytes=VMEM_LIMIT),
        name="samp_attn",
    )(qd, qs, kdn, ksn, vdn, vsn, kdm, ksm, vdm, vsm, ckd, cks, cvd, cvs, lq1, lk1, lq2, lk2, g_head)


def _mlp_kernel(x_ref, md_ref, ms_ref, wo_ref, gm_ref, wu_ref, wd_ref, gf_ref, o_ref,
                x1_ref, h2_ref, acc_ref):
    f = pl.program_id(1)

    @pl.when(f == 0)
    def _():
        x1 = (x_ref[...] + _dot(md_ref[...], wo_ref[:DIFF_WIDTH, :])
              + _dot(ms_ref[...], wo_ref[DIFF_WIDTH:, :]))
        x1_ref[...] = x1
        h2_ref[...] = _rms(x1, gm_ref[...]).astype(_bf16)
        acc_ref[...] = jnp.zeros(acc_ref.shape, _f32)

    u = jnp.maximum(_dot(h2_ref[...], wu_ref[...]), 0.0)
    acc_ref[...] += _dot((u * u).astype(_bf16), wd_ref[...])

    @pl.when(f == pl.num_programs(1) - 1)
    def _():
        o_ref[...] = _rms(x1_ref[...] + acc_ref[...], gf_ref[...])


def _mlp(x, mixed_d, mixed_s, wo, g_mlp, wu, wd, g_final, tm=512, tf=1024):
    rows = x.shape[0]
    small = lambda a: pl.BlockSpec(a.shape, lambda i, f: (0, 0))
    return pl.pallas_call(
        _mlp_kernel,
        grid=(rows // tm, D_FF // tf),
        in_specs=[pl.BlockSpec((tm, D_MODEL), lambda i, f: (i, 0)),
                  pl.BlockSpec((tm, DIFF_WIDTH), lambda i, f: (i, 0)),
                  pl.BlockSpec((tm, SB_WIDTH), lambda i, f: (i, 0)),
                  small(wo), small(g_mlp),
                  pl.BlockSpec((D_MODEL, tf), lambda i, f: (0, f)),
                  pl.BlockSpec((tf, D_MODEL), lambda i, f: (f, 0)),
                  small(g_final)],
        out_specs=pl.BlockSpec((tm, D_MODEL), lambda i, f: (i, 0)),
        out_shape=jax.ShapeDtypeStruct((rows, D_MODEL), _f32),
        scratch_shapes=[pltpu.VMEM((tm, D_MODEL), _f32), pltpu.VMEM((tm, D_MODEL), _bf16),
                        pltpu.VMEM((tm, D_MODEL), _f32)],
        compiler_params=pltpu.CompilerParams(
            dimension_semantics=("arbitrary", "arbitrary"), vmem_limit_bytes=VMEM_LIMIT),
        name="mlp",
    )(x, mixed_d, mixed_s, wo, g_mlp, wu, wd, g_final)


def kernel(x_prompt, x_sample, cache_diff_k, cache_diff_v, cache_sb_k, cache_sb_v, meta_tokens, g_mix, w_in, lambda_q1, lambda_k1, lambda_q2, lambda_k2, g_diff_head, w_out, g_mlp, w_up, w_down, g_final):
    bp, tp, d = x_prompt.shape
    bs, ts, _ = x_sample.shape
    past = cache_diff_k.shape[2]
    assert bp == 1 and g_mix.shape[0] == 1, "single prompt sequence, depth 1"

    w_in_b = w_in[0].astype(_bf16)
    w_out_b = w_out[0].astype(_bf16)
    w_up_b = w_up[0].astype(_bf16)
    w_down_b = w_down[0].astype(_bf16)
    g_final2 = g_final[None, :]
    lams = (lambda_q1, lambda_k1, lambda_q2, lambda_k2)

    cos_p, sin_p = _rope_tables(N_META + jnp.arange(tp))
    p_out = _proj(x_prompt[0], g_mix, w_in_b, cos_p, sin_p, tm=512)
    qd, qs, kd, ks, vd, vs, kdb, ksb, vdb, vsb = p_out

    x_ms = jnp.concatenate([meta_tokens.astype(x_sample.dtype), x_sample.reshape(bs * ts, d)], axis=0)
    pos_ms = jnp.concatenate([jnp.arange(N_META), jnp.tile(N_META + past + jnp.arange(ts), bs)])
    cos_s, sin_s = _rope_tables(pos_ms)
    s_out = _proj(x_ms, g_mix, w_in_b, cos_s, sin_s, tm=x_ms.shape[0])
    meta_f32 = [a[:N_META] for a in s_out[2:6]]
    samp_f32 = [a[N_META:] for a in s_out[2:6]]
    pad_meta = lambda a: jnp.pad(a[:N_META], ((0, META_PAD - N_META), (0, 0)))
    kdm, ksm, vdm, vsm = [pad_meta(a) for a in s_out[6:10]]
    qd_s, qs_s = s_out[0][N_META:], s_out[1][N_META:]
    kdn, ksn, vdn, vsn = [a[N_META:] for a in s_out[6:10]]

    mixed_d = _diff_attn(qd, kdb, vdb, kdm, vdm, *lams, g_diff_head)
    mixed_s = _sb_attn(qs, ksb, vsb, ksm, vsm)
    y_prompt = _mlp(x_prompt[0], mixed_d, mixed_s, w_out_b, g_mlp, w_up_b, w_down_b, g_final2)

    cache2d = lambda c: c[0].reshape(bs, past, DIFF_WIDTH)
    smix_d, smix_s = _samp_attn(qd_s, qs_s, kdn, ksn, vdn, vsn, kdm, ksm, vdm, vsm,
                                cache2d(cache_diff_k), cache2d(cache_sb_k),
                                cache2d(cache_diff_v), cache2d(cache_sb_v),
                                *lams, g_diff_head, ts)
    y_sample = _mlp(x_sample.reshape(bs * ts, d), smix_d, smix_s, w_out_b, g_mlp, w_up_b, w_down_b,
                    g_final2)

    def prompt_cache(meta, frames, tail):
        return jnp.concatenate([meta, frames], axis=0).reshape((1, bp, N_META + tp) + tail)

    def sample_cache(a, tail):
        return a.reshape((1, bs, ts) + tail)

    dk_tail, dv_tail = (N_HEADS_DIFF, 2, HEAD_DIM), (N_HEADS_DIFF, 2 * HEAD_DIM)
    sb_tail = (N_HEADS_SB, HEAD_DIM)
    return (y_prompt[None], y_sample.reshape(bs, ts, d),
            prompt_cache(meta_f32[0], kd, dk_tail), prompt_cache(meta_f32[2], vd, dv_tail),
            prompt_cache(meta_f32[1], ks, sb_tail), prompt_cache(meta_f32[3], vs, sb_tail),
            sample_cache(samp_f32[0], dk_tail), sample_cache(samp_f32[2], dv_tail),
            sample_cache(samp_f32[1], sb_tail), sample_cache(samp_f32[3], sb_tail))
```

```python
import functools
import math

import jax
import jax.numpy as jnp
from jax import lax
from jax.experimental import pallas as pl
from jax.experimental.pallas import tpu as pltpu

D_MODEL = 1024
CHUNK = 64
CHUNK_SHIFT = 6
N_META = 16
HEAD_DIM = 64
N_HEADS_DIFF = 4
N_HEADS_SB = 8
DIFF_WIDTH = N_HEADS_DIFF * 2 * HEAD_DIM
SB_WIDTH = N_HEADS_SB * HEAD_DIM
MIX_WIDTH = DIFF_WIDTH + SB_WIDTH
D_IN = 3 * MIX_WIDTH
D_FF = 4 * D_MODEL
ROPE_THETA = 10000.0
NORM_EPS = 1e-6
NEG_INF = -1e30
LAMBDA_INIT = 0.8 - 0.6 * math.exp(-0.3 * 0)

LANES = 128
META_PAD = 128
SB_LOG_STOP = -90.0

VMEM_LIMIT = 56 * 1024 * 1024

_f32 = jnp.float32
_bf16 = jnp.bfloat16


def _dot(a, b):
    return jnp.dot(a, b, preferred_element_type=_f32)


def _dot_nt(a, b):
    return lax.dot_general(a, b, (((1,), (1,)), ((), ())), preferred_element_type=_f32)


def _rms(x, g):
    return x * lax.rsqrt(jnp.mean(x * x, axis=-1, keepdims=True) + NORM_EPS) * g


def _log_sigmoid(s):
    return jnp.minimum(s, 0.0) - jnp.log1p(jnp.exp(-jnp.abs(s)))


def _proj_kernel(x_ref, g_ref, w_ref, cos_ref, sin_ref,
                 qd_ref, qs_ref, kd_ref, ks_ref, vd_ref, vs_ref,
                 kdb_ref, ksb_ref, vdb_ref, vsb_ref):
    h = _rms(x_ref[...], g_ref[...]).astype(_bf16)
    cos = cos_ref[...]
    sin = sin_ref[...]
    lane = lax.broadcasted_iota(jnp.int32, cos.shape, 1)
    first_half = (lane & (HEAD_DIM - 1)) < (HEAD_DIM // 2)

    def rope(z):
        parts = []
        for j in range(z.shape[1] // LANES):
            zj = z[:, j * LANES:(j + 1) * LANES]
            swapped = jnp.where(first_half,
                                pltpu.roll(zj, LANES - HEAD_DIM // 2, 1),
                                pltpu.roll(zj, HEAD_DIM // 2, 1))
            parts.append(zj * cos + swapped * sin)
        return jnp.concatenate(parts, axis=1)

    def cols(c):
        return _dot(h, w_ref[:, c * DIFF_WIDTH:(c + 1) * DIFF_WIDTH])

    scale = HEAD_DIM ** -0.5
    qd_ref[...] = (rope(cols(0)) * scale).astype(_bf16)
    qs_ref[...] = (cols(1) * scale).astype(_bf16)
    kd = rope(cols(2))
    kd_ref[...] = kd
    kdb_ref[...] = kd.astype(_bf16)
    ks = cols(3)
    ks_ref[...] = ks
    ksb_ref[...] = ks.astype(_bf16)
    vd = cols(4)
    vd_ref[...] = vd
    vdb_ref[...] = vd.astype(_bf16)
    vs = cols(5)
    vs_ref[...] = vs
    vsb_ref[...] = vs.astype(_bf16)


def _proj(x, g, w_bf16, cos, sin, tm):
    rows = x.shape[0]
    row_blk = lambda width: pl.BlockSpec((tm, width), lambda i: (i, 0))
    full = lambda a: pl.BlockSpec(a.shape, lambda i: (0, 0))
    f32_out = jax.ShapeDtypeStruct((rows, DIFF_WIDTH), _f32)
    b16_out = jax.ShapeDtypeStruct((rows, DIFF_WIDTH), _bf16)
    return pl.pallas_call(
        _proj_kernel,
        grid=(rows // tm,),
        in_specs=[row_blk(D_MODEL), full(g), full(w_bf16), row_blk(LANES), row_blk(LANES)],
        out_specs=[row_blk(DIFF_WIDTH)] * 10,
        out_shape=[b16_out, b16_out, f32_out, f32_out, f32_out, f32_out,
                   b16_out, b16_out, b16_out, b16_out],
        compiler_params=pltpu.CompilerParams(
            dimension_semantics=("arbitrary",), vmem_limit_bytes=VMEM_LIMIT),
        name="proj",
    )(x, g, w_bf16, cos, sin)


def _rope_tables(pos):
    half = HEAD_DIM // 2
    inv_freq = ROPE_THETA ** (-jnp.arange(half, dtype=_f32) / half)
    ang = pos.astype(_f32)[:, None] * inv_freq[None, :]
    cos = jnp.cos(ang)
    sin = jnp.sin(ang)
    reps = LANES // HEAD_DIM
    return (jnp.concatenate([cos, cos] * reps, axis=1),
            jnp.concatenate([-sin, sin] * reps, axis=1))


def _lane_half_masks(shape):
    lane = lax.broadcasted_iota(jnp.int32, shape, 1)
    return lane < HEAD_DIM, lane >= HEAD_DIM


def _lambda(lq1, lk1, lq2, lk2):
    return (jnp.exp(jnp.sum(lq1 * lk1, axis=-1, keepdims=True))
            - jnp.exp(jnp.sum(lq2 * lk2, axis=-1, keepdims=True)) + LAMBDA_INIT)


def _diff_finish(o0, l0, o1, l1, lam, g_head):
    o = o0 / l0 - lam * (o1 / l1)
    return _rms(o, g_head) * (1.0 - LAMBDA_INIT)


def _diff_kernel(q_ref, k_ref, v_ref, km_ref, vm_ref, lq1_ref, lk1_ref, lq2_ref, lk2_ref, gh_ref,
                 o_ref, m_ref, l_ref, acc_ref, *, bq, bk, heads):
    qi = pl.program_id(1)
    lo, hi = _lane_half_masks((bq, LANES))
    zero = jnp.zeros((bq, LANES), _bf16)
    chains = []
    for a in range(heads):
        sl = slice(a * LANES, (a + 1) * LANES)
        q = q_ref[:, sl]
        chains.append((2 * a, sl, jnp.where(lo, q, zero)))
        chains.append((2 * a + 1, sl, jnp.where(hi, q, zero)))

    def step(k_of, v_of, mask, first):
        scores = [_dot_nt(qz, k_of(sl)) for _, sl, qz in chains]
        for (c, sl, qz), s in zip(chains, scores):
            if mask is not None:
                s = jnp.where(mask, s, NEG_INF)
            m_cur = jnp.max(s, axis=-1, keepdims=True)
            m_new = jnp.broadcast_to(m_cur, (bq, LANES)) if first else jnp.maximum(m_ref[c], m_cur)
            p = jnp.exp(s - jnp.tile(m_new, (1, s.shape[1] // LANES)))
            l_cur = jnp.sum(p, axis=-1, keepdims=True)
            pv = _dot(p.astype(_bf16), v_of(sl))
            if first:
                l_ref[c] = jnp.broadcast_to(l_cur, (bq, LANES))
                acc_ref[c] = pv
            else:
                alpha = jnp.exp(m_ref[c] - m_new)
                l_ref[c] = alpha * l_ref[c] + l_cur
                acc_ref[c] = alpha * acc_ref[c] + pv
            m_ref[c] = m_new

    mcol = lax.broadcasted_iota(jnp.int32, (bq, META_PAD), 1)
    step(lambda sl: km_ref[:, sl], lambda sl: vm_ref[:, sl], mcol < N_META, True)

    q0 = qi * bq
    n_full = (q0 + CHUNK) // bk

    def tile_at(start, mask):
        step(lambda sl: k_ref[pl.ds(start, bk), sl], lambda sl: v_ref[pl.ds(start, bk), sl], mask, False)

    def full_tile(j, carry):
        tile_at(pl.multiple_of(j * bk, bk), None)
        return carry

    lax.fori_loop(0, n_full, full_tile, 0)

    start = pl.multiple_of(n_full * bk, bk)
    row = lax.broadcasted_iota(jnp.int32, (bq, bk), 0) + q0
    col = lax.broadcasted_iota(jnp.int32, (bq, bk), 1) + start
    tile_at(start, (col >> CHUNK_SHIFT) <= (row >> CHUNK_SHIFT))

    lam = _lambda(lq1_ref[...], lk1_ref[...], lq2_ref[...], lk2_ref[...])
    for a in range(heads):
        out = _diff_finish(acc_ref[2 * a], l_ref[2 * a], acc_ref[2 * a + 1], l_ref[2 * a + 1],
                           lam, gh_ref[...])
        o_ref[:, a * LANES:(a + 1) * LANES] = out.astype(_bf16)


def _diff_attn(qd, kdb, vdb, kmeta, vmeta, lq1, lk1, lq2, lk2, g_head, bq=256, bk=512, heads=2):
    n = qd.shape[0]
    width = heads * LANES
    head_blk = lambda rows: pl.BlockSpec((rows, width), lambda h, i: (0, h))
    small = lambda a: pl.BlockSpec(a.shape, lambda h, i: (0, 0))
    stats = pltpu.VMEM((2 * heads, bq, LANES), _f32)
    return pl.pallas_call(
        functools.partial(_diff_kernel, bq=bq, bk=bk, heads=heads),
        grid=(N_HEADS_DIFF // heads, n // bq),
        in_specs=[pl.BlockSpec((bq, width), lambda h, i: (i, h)),
                  head_blk(n), head_blk(n), head_blk(META_PAD), head_blk(META_PAD),
                  small(lq1), small(lk1), small(lq2), small(lk2), small(g_head)],
        out_specs=pl.BlockSpec((bq, width), lambda h, i: (i, h)),
        out_shape=jax.ShapeDtypeStruct((n, DIFF_WIDTH), _bf16),
        scratch_shapes=[stats, stats, stats],
        compiler_params=pltpu.CompilerParams(
            dimension_semantics=("arbitrary", "arbitrary"), vmem_limit_bytes=VMEM_LIMIT),
        name="diff_attn",
    )(qd, kdb, vdb, kmeta, vmeta, lq1, lk1, lq2, lk2, g_head)


def _suffix_matrix(n):
    j = lax.broadcasted_iota(jnp.int32, (n, n), 0)
    s = lax.broadcasted_iota(jnp.int32, (n, n), 1)
    return jnp.where(j > s, 1.0, 0.0).astype(_bf16)


def _sb_tile(qz, k, v, carry, mask, suffix):
    s = _dot_nt(qz, k)
    log_beta = _log_sigmoid(s)
    log_keep = log_beta - s
    if mask is not None:
        log_keep = jnp.where(mask, log_keep, 0.0)
    keep_hi = log_keep.astype(_bf16)
    keep_lo = (log_keep - keep_hi.astype(_f32)).astype(_bf16)
    after = _dot(keep_hi, suffix) + _dot(keep_lo, suffix)
    a = jnp.exp(log_beta + after + carry)
    if mask is not None:
        a = jnp.where(mask, a, 0.0)
    total = after[:, 0:1] + log_keep[:, 0:1]
    return _dot(a.astype(_bf16), v), carry + total


def _sb_kernel(q_ref, k_ref, v_ref, km_ref, vm_ref, o_ref, acc_ref, carry_ref, *, bq):
    qi = pl.program_id(1)
    q = q_ref[...]
    lo, hi = _lane_half_masks(q.shape)
    zero = jnp.zeros_like(q)
    suffix = _suffix_matrix(bq)
    q0 = qi * bq
    row = lax.broadcasted_iota(jnp.int32, (bq, bq), 0)
    col = lax.broadcasted_iota(jnp.int32, (bq, bq), 1)
    mcol = lax.broadcasted_iota(jnp.int32, (bq, META_PAD), 1)

    outs = []
    for head_mask in (lo, hi):
        qz = jnp.where(head_mask, q, zero)
        start = pl.multiple_of(q0, bq)
        pv, carry = _sb_tile(qz, k_ref[pl.ds(start, bq), :], v_ref[pl.ds(start, bq), :],
                             jnp.zeros((bq, 1), _f32), col < row, suffix)
        acc_ref[...] = pv
        carry_ref[...] = carry

        def cond(state):
            j, live = state
            return jnp.logical_and(j >= 0, live > SB_LOG_STOP)

        def body(state):
            j, _ = state
            start = pl.multiple_of(j * bq, bq)
            pv, carry = _sb_tile(qz, k_ref[pl.ds(start, bq), :], v_ref[pl.ds(start, bq), :],
                                 carry_ref[...], None, suffix)
            acc_ref[...] += pv
            carry_ref[...] = carry
            return j - 1, jnp.max(carry)

        _, live = lax.while_loop(cond, body, (qi - 1, jnp.max(carry)))

        @pl.when(live > SB_LOG_STOP)
        def _():
            pv, _ = _sb_tile(qz, km_ref[...], vm_ref[...], carry_ref[...], mcol < N_META,
                             _suffix_matrix(META_PAD))
            acc_ref[...] += pv

        outs.append(acc_ref[...])
    o_ref[...] = jnp.where(lo, outs[0], outs[1]).astype(_bf16)


def _sb_attn(qs, ksb, vsb, kmeta, vmeta, bq=128):
    n = qs.shape[0]
    pair_blk = lambda rows: pl.BlockSpec((rows, LANES), lambda h, i: (0, h))
    return pl.pallas_call(
        functools.partial(_sb_kernel, bq=bq),
        grid=(SB_WIDTH // LANES, n // bq),
        in_specs=[pl.BlockSpec((bq, LANES), lambda h, i: (i, h)),
                  pair_blk(n), pair_blk(n), pair_blk(META_PAD), pair_blk(META_PAD)],
        out_specs=pl.BlockSpec((bq, LANES), lambda h, i: (i, h)),
        out_shape=jax.ShapeDtypeStruct((n, SB_WIDTH), _bf16),
        scratch_shapes=[pltpu.VMEM((bq, LANES), _f32), pltpu.VMEM((bq, 1), _f32)],
        compiler_params=pltpu.CompilerParams(
            dimension_semantics=("arbitrary", "arbitrary"), vmem_limit_bytes=VMEM_LIMIT),
        name="sb_attn",
    )(qs, ksb, vsb, kmeta, vmeta)


def _samp_kernel(qd_ref, qs_ref, kdn_ref, ksn_ref, vdn_ref, vsn_ref,
                 kdm_ref, ksm_ref, vdm_ref, vsm_ref,
                 ckd_ref, cks_ref, cvd_ref, cvs_ref,
                 lq1_ref, lk1_ref, lq2_ref, lk2_ref, gh_ref,
                 od_ref, os_ref, kd_cat, ks_cat, vd_cat, vs_cat, *, ts, past, n_cat):
    new0 = N_META + past
    for cat, meta, cache, new in ((kd_cat, kdm_ref, ckd_ref, kdn_ref), (ks_cat, ksm_ref, cks_ref, ksn_ref),
                                  (vd_cat, vdm_ref, cvd_ref, vdn_ref), (vs_cat, vsm_ref, cvs_ref, vsn_ref)):
        cat[0:N_META, :] = meta[0:N_META, :]
        cat[N_META:new0, :] = cache[0].astype(_bf16)
        cat[new0:new0 + ts, :] = new[...]
        cat[new0 + ts:, :] = jnp.zeros((n_cat - new0 - ts, DIFF_WIDTH), _bf16)

    pos_k = lax.broadcasted_iota(jnp.int32, (ts, n_cat), 1)
    pos_q = lax.broadcasted_iota(jnp.int32, (ts, n_cat), 0) + new0
    chunk = lambda p: jnp.where(p < N_META, -1, (p - N_META) >> CHUNK_SHIFT)
    mask_chunk = jnp.logical_and(pos_k < new0 + ts, chunk(pos_k) <= chunk(pos_q))
    tile_k = lax.broadcasted_iota(jnp.int32, (ts, LANES), 1)
    tile_q = lax.broadcasted_iota(jnp.int32, (ts, LANES), 0) + new0

    lam = _lambda(lq1_ref[...], lk1_ref[...], lq2_ref[...], lk2_ref[...])
    lo, hi = _lane_half_masks((ts, LANES))
    zero = jnp.zeros((ts, LANES), _bf16)

    for h in range(N_HEADS_DIFF):
        sl = slice(h * LANES, (h + 1) * LANES)
        q = qd_ref[:, sl]
        k = kd_cat[:, sl]
        v = vd_cat[:, sl]
        parts = []
        for half in (lo, hi):
            s = jnp.where(mask_chunk, _dot_nt(jnp.where(half, q, zero), k), NEG_INF)
            p = jnp.exp(s - jnp.max(s, axis=-1, keepdims=True))
            parts.append((_dot(p.astype(_bf16), v), jnp.sum(p, axis=-1, keepdims=True)))
        out = _diff_finish(parts[0][0], parts[0][1], parts[1][0], parts[1][1], lam, gh_ref[...])
        od_ref[:, sl] = out.astype(_bf16)

    suffix = _suffix_matrix(LANES)
    n_tiles = n_cat // LANES
    for hp in range(SB_WIDTH // LANES):
        sl = slice(hp * LANES, (hp + 1) * LANES)
        q = qs_ref[:, sl]
        outs = []
        for half in (lo, hi):
            qz = jnp.where(half, q, zero)
            carry = jnp.zeros((ts, 1), _f32)
            acc = jnp.zeros((ts, LANES), _f32)
            for t in reversed(range(n_tiles)):
                rows = slice(t * LANES, (t + 1) * LANES)
                pv, carry = _sb_tile(qz, ks_cat[rows, sl], vs_cat[rows, sl], carry,
                                     tile_k + t * LANES < tile_q, suffix)
                acc = acc + pv
            outs.append(acc)
        os_ref[:, sl] = jnp.where(lo, outs[0], outs[1]).astype(_bf16)


def _samp_attn(qd, qs, kdn, ksn, vdn, vsn, kdm, ksm, vdm, vsm, ckd, cks, cvd, cvs,
               lq1, lk1, lq2, lk2, g_head, ts):
    nb, past, _ = ckd.shape
    n_cat = -(-(N_META + past + ts) // LANES) * LANES
    row_blk = pl.BlockSpec((ts, DIFF_WIDTH), lambda b: (b, 0))
    cache_blk = pl.BlockSpec((1, past, DIFF_WIDTH), lambda b: (b, 0, 0))
    small = lambda a: pl.BlockSpec(a.shape, lambda b: (0, 0))
    out = jax.ShapeDtypeStruct((nb * ts, DIFF_WIDTH), _bf16)
    return pl.pallas_call(
        functools.partial(_samp_kernel, ts=ts, past=past, n_cat=n_cat),
        grid=(nb,),
        in_specs=[row_blk] * 6 + [small(kdm)] * 4 + [cache_blk] * 4
                 + [small(lq1), small(lk1), small(lq2), small(lk2), small(g_head)],
        out_specs=[row_blk, row_blk],
        out_shape=[out, out],
        scratch_shapes=[pltpu.VMEM((n_cat, DIFF_WIDTH), _bf16)] * 4,
        compiler_params=pltpu.CompilerParams(
            dimension_semantics=("arbitrary",), vmem_limit_bytes=VMEM_LIMIT),
        name="samp_attn",
    )(qd, qs, kdn, ksn, vdn, vsn, kdm, ksm, vdm, vsm, ckd, cks, cvd, cvs, lq1, lk1, lq2, lk2, g_head)


def _mlp_kernel(x_ref, md_ref, ms_ref, wo_ref, gm_ref, wu_ref, wd_ref, gf_ref, o_ref,
                x1_ref, h2_ref, acc_ref):
    f = pl.program_id(1)

    @pl.when(f == 0)
    def _():
        x1 = (x_ref[...] + _dot(md_ref[...], wo_ref[:DIFF_WIDTH, :])
              + _dot(ms_ref[...], wo_ref[DIFF_WIDTH:, :]))
        x1_ref[...] = x1
        h2_ref[...] = _rms(x1, gm_ref[...]).astype(_bf16)
        acc_ref[...] = jnp.zeros(acc_ref.shape, _f32)

    u = jnp.maximum(_dot(h2_ref[...], wu_ref[...]), 0.0)
    acc_ref[...] += _dot((u * u).astype(_bf16), wd_ref[...])

    @pl.when(f == pl.num_programs(1) - 1)
    def _():
        o_ref[...] = _rms(x1_ref[...] + acc_ref[...], gf_ref[...])


def _mlp(x, mixed_d, mixed_s, wo, g_mlp, wu, wd, g_final, tm=512, tf=1024):
    rows = x.shape[0]
    small = lambda a: pl.BlockSpec(a.shape, lambda i, f: (0, 0))
    return pl.pallas_call(
        _mlp_kernel,
        grid=(rows // tm, D_FF // tf),
        in_specs=[pl.BlockSpec((tm, D_MODEL), lambda i, f: (i, 0)),
                  pl.BlockSpec((tm, DIFF_WIDTH), lambda i, f: (i, 0)),
                  pl.BlockSpec((tm, SB_WIDTH), lambda i, f: (i, 0)),
                  small(wo), small(g_mlp),
                  pl.BlockSpec((D_MODEL, tf), lambda i, f: (0, f)),
                  pl.BlockSpec((tf, D_MODEL), lambda i, f: (f, 0)),
                  small(g_final)],
        out_specs=pl.BlockSpec((tm, D_MODEL), lambda i, f: (i, 0)),
        out_shape=jax.ShapeDtypeStruct((rows, D_MODEL), _f32),
        scratch_shapes=[pltpu.VMEM((tm, D_MODEL), _f32), pltpu.VMEM((tm, D_MODEL), _bf16),
                        pltpu.VMEM((tm, D_MODEL), _f32)],
        compiler_params=pltpu.CompilerParams(
            dimension_semantics=("arbitrary", "arbitrary"), vmem_limit_bytes=VMEM_LIMIT),
        name="mlp",
    )(x, mixed_d, mixed_s, wo, g_mlp, wu, wd, g_final)


def kernel(x_prompt, x_sample, cache_diff_k, cache_diff_v, cache_sb_k, cache_sb_v, meta_tokens, g_mix, w_in, lambda_q1, lambda_k1, lambda_q2, lambda_k2, g_diff_head, w_out, g_mlp, w_up, w_down, g_final):
    bp, tp, d = x_prompt.shape
    bs, ts, _ = x_sample.shape
    past = cache_diff_k.shape[2]
    assert bp == 1 and g_mix.shape[0] == 1, "single prompt sequence, depth 1"

    w_in_b = w_in[0].astype(_bf16)
    w_out_b = w_out[0].astype(_bf16)
    w_up_b = w_up[0].astype(_bf16)
    w_down_b = w_down[0].astype(_bf16)
    g_final2 = g_final[None, :]
    lams = (lambda_q1, lambda_k1, lambda_q2, lambda_k2)

    cos_p, sin_p = _rope_tables(N_META + jnp.arange(tp))
    p_out = _proj(x_prompt[0], g_mix, w_in_b, cos_p, sin_p, tm=512)
    qd, qs, kd, ks, vd, vs, kdb, ksb, vdb, vsb = p_out

    x_ms = jnp.concatenate([meta_tokens.astype(x_sample.dtype), x_sample.reshape(bs * ts, d)], axis=0)
    pos_ms = jnp.concatenate([jnp.arange(N_META), jnp.tile(N_META + past + jnp.arange(ts), bs)])
    cos_s, sin_s = _rope_tables(pos_ms)
    s_out = _proj(x_ms, g_mix, w_in_b, cos_s, sin_s, tm=x_ms.shape[0])
    meta_f32 = [a[:N_META] for a in s_out[2:6]]
    samp_f32 = [a[N_META:] for a in s_out[2:6]]
    pad_meta = lambda a: jnp.pad(a[:N_META], ((0, META_PAD - N_META), (0, 0)))
    kdm, ksm, vdm, vsm = [pad_meta(a) for a in s_out[6:10]]
    qd_s, qs_s = s_out[0][N_META:], s_out[1][N_META:]
    kdn, ksn, vdn, vsn = [a[N_META:] for a in s_out[6:10]]

    mixed_d = _diff_attn(qd, kdb, vdb, kdm, vdm, *lams, g_diff_head)
    mixed_s = _sb_attn(qs, ksb, vsb, ksm, vsm)
    y_prompt = _mlp(x_prompt[0], mixed_d, mixed_s, w_out_b, g_mlp, w_up_b, w_down_b, g_final2)

    cache2d = lambda c: c[0].reshape(bs, past, DIFF_WIDTH)
    smix_d, smix_s = _samp_attn(qd_s, qs_s, kdn, ksn, vdn, vsn, kdm, ksm, vdm, vsm,
                                cache2d(cache_diff_k), cache2d(cache_sb_k),
                                cache2d(cache_diff_v), cache2d(cache_sb_v),
                                *lams, g_diff_head, ts)
    y_sample = _mlp(x_sample.reshape(bs * ts, d), smix_d, smix_s, w_out_b, g_mlp, w_up_b, w_down_b,
                    g_final2)

    def prompt_cache(meta, frames, tail):
        return jnp.concatenate([meta, frames], axis=0).reshape((1, bp, N_META + tp) + tail)

    def sample_cache(a, tail):
        return a.reshape((1, bs, ts) + tail)

    dk_tail, dv_tail = (N_HEADS_DIFF, 2, HEAD_DIM), (N_HEADS_DIFF, 2 * HEAD_DIM)
    sb_tail = (N_HEADS_SB, HEAD_DIM)
    return (y_prompt[None], y_sample.reshape(bs, ts, d),
            prompt_cache(meta_f32[0], kd, dk_tail), prompt_cache(meta_f32[2], vd, dv_tail),
            prompt_cache(meta_f32[1], ks, sb_tail), prompt_cache(meta_f32[3], vs, sb_tail),
            sample_cache(samp_f32[0], dk_tail), sample_cache(samp_f32[2], dv_tail),
            sample_cache(samp_f32[1], sb_tail), sample_cache(samp_f32[3], sb_tail))
```

```python
import functools
import math

import jax
import jax.numpy as jnp
from jax import lax
from jax.experimental import pallas as pl
from jax.experimental.pallas import tpu as pltpu

D_MODEL = 1024
CHUNK = 64
CHUNK_SHIFT = 6
N_META = 16
HEAD_DIM = 64
N_HEADS_DIFF = 4
N_HEADS_SB = 8
DIFF_WIDTH = N_HEADS_DIFF * 2 * HEAD_DIM
SB_WIDTH = N_HEADS_SB * HEAD_DIM
MIX_WIDTH = DIFF_WIDTH + SB_WIDTH
D_IN = 3 * MIX_WIDTH
D_FF = 4 * D_MODEL
ROPE_THETA = 10000.0
NORM_EPS = 1e-6
NEG_INF = -1e30
LAMBDA_INIT = 0.8 - 0.6 * math.exp(-0.3 * 0)
LOG2_E = math.log2(math.e)

LANES = 128
META_PAD = 128
SB_LOG_STOP = -90.0

VMEM_LIMIT = 56 * 1024 * 1024

_f32 = jnp.float32
_bf16 = jnp.bfloat16


def _dot(a, b):
    return jnp.dot(a, b, preferred_element_type=_f32)


def _dot_nt(a, b):
    return lax.dot_general(a, b, (((1,), (1,)), ((), ())), preferred_element_type=_f32)


def _rms(x, g):
    return x * lax.rsqrt(jnp.mean(x * x, axis=-1, keepdims=True) + NORM_EPS) * g


def _log_sigmoid(s):
    return jnp.minimum(s, 0.0) - jnp.log(1.0 + jnp.exp(-jnp.abs(s)))


def _proj_kernel(x_ref, g_ref, w_ref, cos_ref, sin_ref,
                 qd_ref, qs_ref, kd_ref, ks_ref, vd_ref, vs_ref,
                 kdb_ref, ksb_ref, vdb_ref, vsb_ref):
    h = _rms(x_ref[...], g_ref[...]).astype(_bf16)
    cos = cos_ref[...]
    sin = sin_ref[...]
    lane = lax.broadcasted_iota(jnp.int32, cos.shape, 1)
    first_half = (lane & (HEAD_DIM - 1)) < (HEAD_DIM // 2)

    def rope(z):
        parts = []
        for j in range(z.shape[1] // LANES):
            zj = z[:, j * LANES:(j + 1) * LANES]
            swapped = jnp.where(first_half,
                                pltpu.roll(zj, LANES - HEAD_DIM // 2, 1),
                                pltpu.roll(zj, HEAD_DIM // 2, 1))
            parts.append(zj * cos + swapped * sin)
        return jnp.concatenate(parts, axis=1)

    def cols(c):
        return _dot(h, w_ref[:, c * DIFF_WIDTH:(c + 1) * DIFF_WIDTH])

    scale = HEAD_DIM ** -0.5
    qd_ref[...] = (rope(cols(0)) * (scale * LOG2_E)).astype(_bf16)
    qs_ref[...] = (cols(1) * scale).astype(_bf16)
    kd = rope(cols(2))
    kd_ref[...] = kd
    kdb_ref[...] = kd.astype(_bf16)
    ks = cols(3)
    ks_ref[...] = ks
    ksb_ref[...] = ks.astype(_bf16)
    vd = cols(4)
    vd_ref[...] = vd
    vdb_ref[...] = vd.astype(_bf16)
    vs = cols(5)
    vs_ref[...] = vs
    vsb_ref[...] = vs.astype(_bf16)


def _proj(x, g, w_bf16, cos, sin, tm):
    rows = x.shape[0]
    row_blk = lambda width: pl.BlockSpec((tm, width), lambda i: (i, 0))
    full = lambda a: pl.BlockSpec(a.shape, lambda i: (0, 0))
    f32_out = jax.ShapeDtypeStruct((rows, DIFF_WIDTH), _f32)
    b16_out = jax.ShapeDtypeStruct((rows, DIFF_WIDTH), _bf16)
    return pl.pallas_call(
        _proj_kernel,
        grid=(rows // tm,),
        in_specs=[row_blk(D_MODEL), full(g), full(w_bf16), row_blk(LANES), row_blk(LANES)],
        out_specs=[row_blk(DIFF_WIDTH)] * 10,
        out_shape=[b16_out, b16_out, f32_out, f32_out, f32_out, f32_out,
                   b16_out, b16_out, b16_out, b16_out],
        compiler_params=pltpu.CompilerParams(
            dimension_semantics=("arbitrary",), vmem_limit_bytes=VMEM_LIMIT),
        name="proj",
    )(x, g, w_bf16, cos, sin)


def _rope_tables(pos):
    half = HEAD_DIM // 2
    inv_freq = ROPE_THETA ** (-jnp.arange(half, dtype=_f32) / half)
    ang = pos.astype(_f32)[:, None] * inv_freq[None, :]
    cos = jnp.cos(ang)
    sin = jnp.sin(ang)
    reps = LANES // HEAD_DIM
    return (jnp.concatenate([cos, cos] * reps, axis=1),
            jnp.concatenate([-sin, sin] * reps, axis=1))


def _lane_half_masks(shape):
    lane = lax.broadcasted_iota(jnp.int32, shape, 1)
    return lane < HEAD_DIM, lane >= HEAD_DIM


def _lambda(lq1, lk1, lq2, lk2):
    return (jnp.exp(jnp.sum(lq1 * lk1, axis=-1, keepdims=True))
            - jnp.exp(jnp.sum(lq2 * lk2, axis=-1, keepdims=True)) + LAMBDA_INIT)


def _diff_finish(o0, l0, o1, l1, lam, g_head):
    o = o0 / l0 - lam * (o1 / l1)
    return _rms(o, g_head) * (1.0 - LAMBDA_INIT)


def _diff_kernel(q_ref, k_ref, v_ref, km_ref, vm_ref, lq1_ref, lk1_ref, lq2_ref, lk2_ref, gh_ref,
                 o_ref, m_ref, l_ref, acc_ref, *, bq, bk, heads):
    qi = pl.program_id(1)
    lo, hi = _lane_half_masks((bq, LANES))
    zero = jnp.zeros((bq, LANES), _bf16)
    chains = []
    for a in range(heads):
        sl = slice(a * LANES, (a + 1) * LANES)
        q = q_ref[:, sl]
        chains.append((2 * a, sl, jnp.where(lo, q, zero)))
        chains.append((2 * a + 1, sl, jnp.where(hi, q, zero)))

    def step(k_of, v_of, mask, first):
        scores = [_dot_nt(qz, k_of(sl)) for _, sl, qz in chains]
        for (c, sl, qz), s in zip(chains, scores):
            if mask is not None:
                s = jnp.where(mask, s, NEG_INF)
            m_cur = jnp.max(s, axis=-1, keepdims=True)
            m_new = jnp.broadcast_to(m_cur, (bq, LANES)) if first else jnp.maximum(m_ref[c], m_cur)
            p = jnp.exp2(s - jnp.tile(m_new, (1, s.shape[1] // LANES)))
            l_cur = jnp.sum(p, axis=-1, keepdims=True)
            pv = _dot(p.astype(_bf16), v_of(sl))
            if first:
                l_ref[c] = jnp.broadcast_to(l_cur, (bq, LANES))
                acc_ref[c] = pv
            else:
                alpha = jnp.exp2(m_ref[c] - m_new)
                l_ref[c] = alpha * l_ref[c] + l_cur
                acc_ref[c] = alpha * acc_ref[c] + pv
            m_ref[c] = m_new

    mcol = lax.broadcasted_iota(jnp.int32, (bq, META_PAD), 1)
    step(lambda sl: km_ref[:, sl], lambda sl: vm_ref[:, sl], mcol < N_META, True)

    q0 = qi * bq
    n_full = (q0 + CHUNK) // bk

    def tile_at(start, mask):
        step(lambda sl: k_ref[pl.ds(start, bk), sl], lambda sl: v_ref[pl.ds(start, bk), sl], mask, False)

    def full_tile(j, carry):
        tile_at(pl.multiple_of(j * bk, bk), None)
        return carry

    lax.fori_loop(0, n_full, full_tile, 0)

    start = pl.multiple_of(n_full * bk, bk)
    row = lax.broadcasted_iota(jnp.int32, (bq, bk), 0) + q0
    col = lax.broadcasted_iota(jnp.int32, (bq, bk), 1) + start
    tile_at(start, (col >> CHUNK_SHIFT) <= (row >> CHUNK_SHIFT))

    lam = _lambda(lq1_ref[...], lk1_ref[...], lq2_ref[...], lk2_ref[...])
    for a in range(heads):
        out = _diff_finish(acc_ref[2 * a], l_ref[2 * a], acc_ref[2 * a + 1], l_ref[2 * a + 1],
                           lam, gh_ref[...])
        o_ref[:, a * LANES:(a + 1) * LANES] = out.astype(_bf16)


def _diff_attn(qd, kdb, vdb, kmeta, vmeta, lq1, lk1, lq2, lk2, g_head, bq=256, bk=512,
               heads=N_HEADS_DIFF):
    n = qd.shape[0]
    width = heads * LANES
    head_blk = lambda rows: pl.BlockSpec((rows, width), lambda h, i: (0, h),
                                         pipeline_mode=pl.Buffered(1))
    small = lambda a: pl.BlockSpec(a.shape, lambda h, i: (0, 0))
    stats = pltpu.VMEM((2 * heads, bq, LANES), _f32)
    return pl.pallas_call(
        functools.partial(_diff_kernel, bq=bq, bk=bk, heads=heads),
        grid=(N_HEADS_DIFF // heads, n // bq),
        in_specs=[pl.BlockSpec((bq, width), lambda h, i: (i, h)),
                  head_blk(n), head_blk(n), head_blk(META_PAD), head_blk(META_PAD),
                  small(lq1), small(lk1), small(lq2), small(lk2), small(g_head)],
        out_specs=pl.BlockSpec((bq, width), lambda h, i: (i, h)),
        out_shape=jax.ShapeDtypeStruct((n, DIFF_WIDTH), _bf16),
        scratch_shapes=[stats, stats, stats],
        compiler_params=pltpu.CompilerParams(
            dimension_semantics=("arbitrary", "arbitrary"), vmem_limit_bytes=VMEM_LIMIT),
        name="diff_attn",
    )(qd, kdb, vdb, kmeta, vmeta, lq1, lk1, lq2, lk2, g_head)


def _suffix_matrix(n):
    j = lax.broadcasted_iota(jnp.int32, (n, n), 0)
    s = lax.broadcasted_iota(jnp.int32, (n, n), 1)
    return jnp.where(j > s, 1.0, 0.0).astype(_bf16)


def _sb_tiles(queries, keys, values, carries, mask, suffix):
    scores = [_dot_nt(q, k) for q, k in zip(queries, keys)]
    log_betas, log_keeps, splits = [], [], []
    for s in scores:
        log_beta = _log_sigmoid(s)
        log_keep = log_beta - s
        if mask is not None:
            log_keep = jnp.where(mask, log_keep, 0.0)
        keep_hi = log_keep.astype(_bf16)
        keep_lo = (log_keep - keep_hi.astype(_f32)).astype(_bf16)
        log_betas.append(log_beta)
        log_keeps.append(log_keep)
        splits.append((keep_hi, keep_lo))
    afters = [_dot(hi, suffix) + _dot(lo, suffix) for hi, lo in splits]
    weights, new_carries = [], []
    for log_beta, log_keep, after, carry in zip(log_betas, log_keeps, afters, carries):
        a = jnp.exp(log_beta + after + carry)
        if mask is not None:
            a = jnp.where(mask, a, 0.0)
        weights.append(a.astype(_bf16))
        new_carries.append(carry + after[:, 0:1] + log_keep[:, 0:1])
    return [_dot(a, v) for a, v in zip(weights, values)], new_carries


def _sb_kernel(q_ref, k_ref, v_ref, km_ref, vm_ref, o_ref, acc_ref, carry_ref, *, bq):
    qi = pl.program_id(0)
    lo, hi = _lane_half_masks((bq, LANES))
    zero = jnp.zeros((bq, LANES), _bf16)
    lanes, queries = [], []
    for p in range(SB_WIDTH // LANES):
        sl = slice(p * LANES, (p + 1) * LANES)
        q = q_ref[:, sl]
        lanes += [sl, sl]
        queries += [jnp.where(lo, q, zero), jnp.where(hi, q, zero)]
    heads = range(N_HEADS_SB)
    suffix = _suffix_matrix(bq)

    def tile(k_of, v_of, mask, suffix, first):
        carries = [jnp.zeros((bq, 1), _f32) if first else carry_ref[c] for c in heads]
        pvs, carries = _sb_tiles(queries, [k_of(sl) for sl in lanes], [v_of(sl) for sl in lanes],
                                 carries, mask, suffix)
        live = carries[0]
        for c in heads:
            acc_ref[c] = pvs[c] if first else acc_ref[c] + pvs[c]
            carry_ref[c] = carries[c]
            live = jnp.maximum(live, carries[c])
        return jnp.max(live)

    def frame_tile(j, mask, first):
        start = pl.multiple_of(j * bq, bq)
        return tile(lambda sl: k_ref[pl.ds(start, bq), sl], lambda sl: v_ref[pl.ds(start, bq), sl],
                    mask, suffix, first)

    row = lax.broadcasted_iota(jnp.int32, (bq, bq), 0)
    col = lax.broadcasted_iota(jnp.int32, (bq, bq), 1)
    live = frame_tile(qi, col < row, True)

    def cond(state):
        j, live = state
        return jnp.logical_and(j >= 0, live > SB_LOG_STOP)

    def body(state):
        j, _ = state
        return j - 1, frame_tile(j, None, False)

    _, live = lax.while_loop(cond, body, (qi - 1, live))

    @pl.when(live > SB_LOG_STOP)
    def _():
        mcol = lax.broadcasted_iota(jnp.int32, (bq, META_PAD), 1)
        tile(lambda sl: km_ref[:, sl], lambda sl: vm_ref[:, sl], mcol < N_META,
             _suffix_matrix(META_PAD), False)

    for p in range(SB_WIDTH // LANES):
        o_ref[:, p * LANES:(p + 1) * LANES] = jnp.where(lo, acc_ref[2 * p], acc_ref[2 * p + 1]).astype(_bf16)


def _sb_attn(qs, ksb, vsb, kmeta, vmeta, bq=128):
    n = qs.shape[0]
    resident = lambda a: pl.BlockSpec(a.shape, lambda i: (0, 0), pipeline_mode=pl.Buffered(1))
    return pl.pallas_call(
        functools.partial(_sb_kernel, bq=bq),
        grid=(n // bq,),
        in_specs=[pl.BlockSpec((bq, SB_WIDTH), lambda i: (i, 0)),
                  resident(ksb), resident(vsb), resident(kmeta), resident(vmeta)],
        out_specs=pl.BlockSpec((bq, SB_WIDTH), lambda i: (i, 0)),
        out_shape=jax.ShapeDtypeStruct((n, SB_WIDTH), _bf16),
        scratch_shapes=[pltpu.VMEM((N_HEADS_SB, bq, LANES), _f32),
                        pltpu.VMEM((N_HEADS_SB, bq, 1), _f32)],
        compiler_params=pltpu.CompilerParams(
            dimension_semantics=("arbitrary",), vmem_limit_bytes=VMEM_LIMIT),
        name="sb_attn",
    )(qs, ksb, vsb, kmeta, vmeta)


def _samp_kernel(qd_ref, qs_ref, kdn_ref, ksn_ref, vdn_ref, vsn_ref,
                 kdm_ref, ksm_ref, vdm_ref, vsm_ref,
                 ckd_ref, cks_ref, cvd_ref, cvs_ref,
                 lq1_ref, lk1_ref, lq2_ref, lk2_ref, gh_ref,
                 od_ref, os_ref, kd_cat, ks_cat, vd_cat, vs_cat, *, ts, past, n_cat):
    new0 = N_META + past
    for cat, meta, cache, new in ((kd_cat, kdm_ref, ckd_ref, kdn_ref), (ks_cat, ksm_ref, cks_ref, ksn_ref),
                                  (vd_cat, vdm_ref, cvd_ref, vdn_ref), (vs_cat, vsm_ref, cvs_ref, vsn_ref)):
        cat[0:N_META, :] = meta[0:N_META, :]
        cat[N_META:new0, :] = cache[0].astype(_bf16)
        cat[new0:new0 + ts, :] = new[...]
        cat[new0 + ts:, :] = jnp.zeros((n_cat - new0 - ts, DIFF_WIDTH), _bf16)

    pos_k = lax.broadcasted_iota(jnp.int32, (2 * ts, n_cat), 1)
    pos_q = (lax.broadcasted_iota(jnp.int32, (2 * ts, n_cat), 0) & (ts - 1)) + new0
    chunk = lambda p: jnp.where(p < N_META, -1, (p - N_META) >> CHUNK_SHIFT)
    mask_chunk = jnp.logical_and(pos_k < new0 + ts, chunk(pos_k) <= chunk(pos_q))
    mask_strict = pos_k < pos_q

    lo, hi = _lane_half_masks((ts, LANES))
    zero = jnp.zeros((ts, LANES), _bf16)
    blocks = [slice(j * LANES, (j + 1) * LANES) for j in range(DIFF_WIDTH // LANES)]

    def stacked(q):
        return jnp.concatenate([jnp.where(lo, q, zero), jnp.where(hi, q, zero)], axis=0)

    diff_scores = [_dot_nt(stacked(qd_ref[:, sl]), kd_cat[:, sl]) for sl in blocks]
    sb_scores = [_dot_nt(stacked(qs_ref[:, sl]), ks_cat[:, sl]) for sl in blocks]

    lam = _lambda(lq1_ref[...], lk1_ref[...], lq2_ref[...], lk2_ref[...])
    probs, sums = [], []
    for s in diff_scores:
        s = jnp.where(mask_chunk, s, NEG_INF)
        p = jnp.exp2(s - jnp.max(s, axis=-1, keepdims=True))
        probs.append(p.astype(_bf16))
        sums.append(jnp.sum(p, axis=-1, keepdims=True))
    for sl, p, l in zip(blocks, probs, sums):
        pv = _dot(p, vd_cat[:, sl])
        out = _diff_finish(pv[:ts], l[:ts], pv[ts:], l[ts:], lam, gh_ref[...])
        od_ref[:, sl] = out.astype(_bf16)

    n_tiles = n_cat // LANES
    tiles = [slice(t * LANES, (t + 1) * LANES) for t in range(n_tiles)]
    log_betas, log_keeps, pieces = [], [], []
    for s in sb_scores:
        log_beta = _log_sigmoid(s)
        log_keep = jnp.where(mask_strict, log_beta - s, 0.0)
        keep_hi = log_keep.astype(_bf16)
        keep_lo = (log_keep - keep_hi.astype(_f32)).astype(_bf16)
        log_betas.append(log_beta)
        log_keeps.append(log_keep)
        pieces += [keep_hi[:, t] for t in tiles] + [keep_lo[:, t] for t in tiles]
    after_all = _dot(jnp.concatenate(pieces, axis=0), _suffix_matrix(LANES))
    grp = 2 * ts
    for j, sl in enumerate(blocks):
        base = j * 2 * n_tiles
        after = [after_all[(base + t) * grp:(base + t + 1) * grp]
                 + after_all[(base + n_tiles + t) * grp:(base + n_tiles + t + 1) * grp]
                 for t in range(n_tiles)]
        carry = jnp.zeros((grp, 1), _f32)
        shifted = [None] * n_tiles
        for t in reversed(range(n_tiles)):
            shifted[t] = after[t] + carry
            carry = carry + after[t][:, 0:1] + log_keeps[j][:, tiles[t]][:, 0:1]
        a = jnp.exp(log_betas[j] + jnp.concatenate(shifted, axis=1))
        a = jnp.where(mask_strict, a, 0.0).astype(_bf16)
        pv = _dot(a, vs_cat[:, sl])
        os_ref[:, sl] = jnp.where(lo, pv[:ts], pv[ts:]).astype(_bf16)


def _samp_attn(qd, qs, kdn, ksn, vdn, vsn, kdm, ksm, vdm, vsm, ckd, cks, cvd, cvs,
               lq1, lk1, lq2, lk2, g_head, ts):
    nb, past, _ = ckd.shape
    n_cat = -(-(N_META + past + ts) // LANES) * LANES
    row_blk = pl.BlockSpec((ts, DIFF_WIDTH), lambda b: (b, 0))
    cache_blk = pl.BlockSpec((1, past, DIFF_WIDTH), lambda b: (b, 0, 0))
    small = lambda a: pl.BlockSpec(a.shape, lambda b: (0, 0))
    out = jax.ShapeDtypeStruct((nb * ts, DIFF_WIDTH), _bf16)
    return pl.pallas_call(
        functools.partial(_samp_kernel, ts=ts, past=past, n_cat=n_cat),
        grid=(nb,),
        in_specs=[row_blk] * 6 + [small(kdm)] * 4 + [cache_blk] * 4
                 + [small(lq1), small(lk1), small(lq2), small(lk2), small(g_head)],
        out_specs=[row_blk, row_blk],
        out_shape=[out, out],
        scratch_shapes=[pltpu.VMEM((n_cat, DIFF_WIDTH), _bf16)] * 4,
        compiler_params=pltpu.CompilerParams(
            dimension_semantics=("arbitrary",), vmem_limit_bytes=VMEM_LIMIT),
        name="samp_attn",
    )(qd, qs, kdn, ksn, vdn, vsn, kdm, ksm, vdm, vsm, ckd, cks, cvd, cvs, lq1, lk1, lq2, lk2, g_head)


def _mlp_kernel(x_ref, md_ref, ms_ref, wo_ref, gm_ref, wu_ref, wd_ref, gf_ref, o_ref,
                x1_ref, h2_ref, acc_ref):
    f = pl.program_id(1)

    @pl.when(f == 0)
    def _():
        x1 = (x_ref[...] + _dot(md_ref[...], wo_ref[:DIFF_WIDTH, :])
              + _dot(ms_ref[...], wo_ref[DIFF_WIDTH:, :]))
        x1_ref[...] = x1
        h2_ref[...] = _rms(x1, gm_ref[...]).astype(_bf16)
        acc_ref[...] = jnp.zeros(acc_ref.shape, _f32)

    u = jnp.maximum(_dot(h2_ref[...], wu_ref[...]), 0.0)
    acc_ref[...] += _dot((u * u).astype(_bf16), wd_ref[...])

    @pl.when(f == pl.num_programs(1) - 1)
    def _():
        o_ref[...] = _rms(x1_ref[...] + acc_ref[...], gf_ref[...])


def _mlp(x, mixed_d, mixed_s, wo, g_mlp, wu, wd, g_final, tm=512, tf=1024):
    rows = x.shape[0]
    small = lambda a: pl.BlockSpec(a.shape, lambda i, f: (0, 0))
    return pl.pallas_call(
        _mlp_kernel,
        grid=(rows // tm, D_FF // tf),
        in_specs=[pl.BlockSpec((tm, D_MODEL), lambda i, f: (i, 0)),
                  pl.BlockSpec((tm, DIFF_WIDTH), lambda i, f: (i, 0)),
                  pl.BlockSpec((tm, SB_WIDTH), lambda i, f: (i, 0)),
                  small(wo), small(g_mlp),
                  pl.BlockSpec((D_MODEL, tf), lambda i, f: (0, f)),
                  pl.BlockSpec((tf, D_MODEL), lambda i, f: (f, 0)),
                  small(g_final)],
        out_specs=pl.BlockSpec((tm, D_MODEL), lambda i, f: (i, 0)),
        out_shape=jax.ShapeDtypeStruct((rows, D_MODEL), _f32),
        scratch_shapes=[pltpu.VMEM((tm, D_MODEL), _f32), pltpu.VMEM((tm, D_MODEL), _bf16),
                        pltpu.VMEM((tm, D_MODEL), _f32)],
        compiler_params=pltpu.CompilerParams(
            dimension_semantics=("arbitrary", "arbitrary"), vmem_limit_bytes=VMEM_LIMIT),
        name="mlp",
    )(x, mixed_d, mixed_s, wo, g_mlp, wu, wd, g_final)


def kernel(x_prompt, x_sample, cache_diff_k, cache_diff_v, cache_sb_k, cache_sb_v, meta_tokens, g_mix, w_in, lambda_q1, lambda_k1, lambda_q2, lambda_k2, g_diff_head, w_out, g_mlp, w_up, w_down, g_final):
    bp, tp, d = x_prompt.shape
    bs, ts, _ = x_sample.shape
    past = cache_diff_k.shape[2]
    assert bp == 1 and g_mix.shape[0] == 1, "single prompt sequence, depth 1"

    w_in_b = w_in[0].astype(_bf16)
    w_out_b = w_out[0].astype(_bf16)
    w_up_b = w_up[0].astype(_bf16)
    w_down_b = w_down[0].astype(_bf16)
    g_final2 = g_final[None, :]
    lams = (lambda_q1, lambda_k1, lambda_q2, lambda_k2)

    cos_p, sin_p = _rope_tables(N_META + jnp.arange(tp))
    p_out = _proj(x_prompt[0], g_mix, w_in_b, cos_p, sin_p, tm=512)
    qd, qs, kd, ks, vd, vs, kdb, ksb, vdb, vsb = p_out

    x_ms = jnp.concatenate([meta_tokens.astype(x_sample.dtype), x_sample.reshape(bs * ts, d)], axis=0)
    pos_ms = jnp.concatenate([jnp.arange(N_META), jnp.tile(N_META + past + jnp.arange(ts), bs)])
    cos_s, sin_s = _rope_tables(pos_ms)
    s_out = _proj(x_ms, g_mix, w_in_b, cos_s, sin_s, tm=x_ms.shape[0])
    meta_f32 = [a[:N_META] for a in s_out[2:6]]
    samp_f32 = [a[N_META:] for a in s_out[2:6]]
    pad_meta = lambda a: jnp.pad(a[:N_META], ((0, META_PAD - N_META), (0, 0)))
    kdm, ksm, vdm, vsm = [pad_meta(a) for a in s_out[6:10]]
    qd_s, qs_s = s_out[0][N_META:], s_out[1][N_META:]
    kdn, ksn, vdn, vsn = [a[N_META:] for a in s_out[6:10]]

    mixed_d = _diff_attn(qd, kdb, vdb, kdm, vdm, *lams, g_diff_head)
    mixed_s = _sb_attn(qs, ksb, vsb, ksm, vsm)
    y_prompt = _mlp(x_prompt[0], mixed_d, mixed_s, w_out_b, g_mlp, w_up_b, w_down_b, g_final2)

    cache2d = lambda c: c[0].reshape(bs, past, DIFF_WIDTH)
    smix_d, smix_s = _samp_attn(qd_s, qs_s, kdn, ksn, vdn, vsn, kdm, ksm, vdm, vsm,
                                cache2d(cache_diff_k), cache2d(cache_sb_k),
                                cache2d(cache_diff_v), cache2d(cache_sb_v),
                                *lams, g_diff_head, ts)
    y_sample = _mlp(x_sample.reshape(bs * ts, d), smix_d, smix_s, w_out_b, g_mlp, w_up_b, w_down_b,
                    g_final2)

    def prompt_cache(meta, frames, tail):
        return jnp.concatenate([meta, frames], axis=0).reshape((1, bp, N_META + tp) + tail)

    def sample_cache(a, tail):
        return a.reshape((1, bs, ts) + tail)

    dk_tail, dv_tail = (N_HEADS_DIFF, 2, HEAD_DIM), (N_HEADS_DIFF, 2 * HEAD_DIM)
    sb_tail = (N_HEADS_SB, HEAD_DIM)
    return (y_prompt[None], y_sample.reshape(bs, ts, d),
            prompt_cache(meta_f32[0], kd, dk_tail), prompt_cache(meta_f32[2], vd, dv_tail),
            prompt_cache(meta_f32[1], ks, sb_tail), prompt_cache(meta_f32[3], vs, sb_tail),
            sample_cache(samp_f32[0], dk_tail), sample_cache(samp_f32[2], dv_tail),
            sample_cache(samp_f32[1], sb_tail), sample_cache(samp_f32[3], sb_tail))
```

```python
import functools
import math

import jax
import jax.numpy as jnp
from jax import lax
from jax.experimental import pallas as pl
from jax.experimental.pallas import tpu as pltpu

D_MODEL = 1024
CHUNK = 64
CHUNK_SHIFT = 6
N_META = 16
HEAD_DIM = 64
N_HEADS_DIFF = 4
N_HEADS_SB = 8
DIFF_WIDTH = N_HEADS_DIFF * 2 * HEAD_DIM
SB_WIDTH = N_HEADS_SB * HEAD_DIM
MIX_WIDTH = DIFF_WIDTH + SB_WIDTH
D_IN = 3 * MIX_WIDTH
D_FF = 4 * D_MODEL
ROPE_THETA = 10000.0
NORM_EPS = 1e-6
NEG_INF = -1e30
LAMBDA_INIT = 0.8 - 0.6 * math.exp(-0.3 * 0)
LOG2_E = math.log2(math.e)

LANES = 128
META_PAD = 128
SB_LOG_STOP = -90.0
SCORE_LEAD = 2

VMEM_LIMIT = 56 * 1024 * 1024

_f32 = jnp.float32
_bf16 = jnp.bfloat16


def _dot(a, b):
    return jnp.dot(a, b, preferred_element_type=_f32)


def _dot_nt(a, b):
    return lax.dot_general(a, b, (((1,), (1,)), ((), ())), preferred_element_type=_f32)


def _rms(x, g):
    return x * lax.rsqrt(jnp.mean(x * x, axis=-1, keepdims=True) + NORM_EPS) * g


def _log_sigmoid(s):
    return jnp.minimum(s, 0.0) - jnp.log(1.0 + jnp.exp(-jnp.abs(s)))


def _proj_math(x_ref, g_ref, w_ref, cos_ref, sin_ref):
    h = _rms(x_ref[...], g_ref[...]).astype(_bf16)
    cos = cos_ref[...]
    sin = sin_ref[...]
    lane = lax.broadcasted_iota(jnp.int32, cos.shape, 1)
    first_half = (lane & (HEAD_DIM - 1)) < (HEAD_DIM // 2)

    def rope(z):
        parts = []
        for j in range(z.shape[1] // LANES):
            zj = z[:, j * LANES:(j + 1) * LANES]
            swapped = jnp.where(first_half,
                                pltpu.roll(zj, LANES - HEAD_DIM // 2, 1),
                                pltpu.roll(zj, HEAD_DIM // 2, 1))
            parts.append(zj * cos + swapped * sin)
        return jnp.concatenate(parts, axis=1)

    def cols(c):
        return _dot(h, w_ref[:, c * DIFF_WIDTH:(c + 1) * DIFF_WIDTH])

    scale = HEAD_DIM ** -0.5
    return (rope(cols(0)) * (scale * LOG2_E), cols(1) * scale, rope(cols(2)), cols(3), cols(4), cols(5))


def _proj_kernel(x_ref, g_ref, w_ref, cos_ref, sin_ref,
                 qd_ref, qs_ref, kd_ref, ks_ref, vd_ref, vs_ref,
                 kdb_ref, ksb_ref, vdb_ref, vsb_ref):
    qd, qs, kd, ks, vd, vs = _proj_math(x_ref, g_ref, w_ref, cos_ref, sin_ref)
    qd_ref[...] = qd.astype(_bf16)
    qs_ref[...] = qs.astype(_bf16)
    for f32_ref, b16_ref, val in ((kd_ref, kdb_ref, kd), (ks_ref, ksb_ref, ks),
                                  (vd_ref, vdb_ref, vd), (vs_ref, vsb_ref, vs)):
        f32_ref[...] = val
        b16_ref[...] = val.astype(_bf16)


def _proj(x, g, w_bf16, cos, sin):
    rows = x.shape[0]
    full = lambda a: pl.BlockSpec(a.shape, lambda i: (0, 0))
    blk = lambda width: pl.BlockSpec((rows, width), lambda i: (0, 0))
    f32_out = jax.ShapeDtypeStruct((rows, DIFF_WIDTH), _f32)
    b16_out = jax.ShapeDtypeStruct((rows, DIFF_WIDTH), _bf16)
    return pl.pallas_call(
        _proj_kernel,
        grid=(1,),
        in_specs=[full(x), full(g), full(w_bf16), full(cos), full(sin)],
        out_specs=[blk(DIFF_WIDTH)] * 10,
        out_shape=[b16_out, b16_out, f32_out, f32_out, f32_out, f32_out,
                   b16_out, b16_out, b16_out, b16_out],
        compiler_params=pltpu.CompilerParams(
            dimension_semantics=("arbitrary",), vmem_limit_bytes=VMEM_LIMIT),
        name="proj",
    )(x, g, w_bf16, cos, sin)


def _proj_prompt_kernel(x_ref, g_ref, w_ref, cos_ref, sin_ref, mkd_ref, mks_ref, mvd_ref, mvs_ref,
                        qd_ref, qs_ref, kd_ref, ks_ref, vd_ref, vs_ref,
                        kdb_ref, ksb_ref, vdb_ref, vsb_ref, tail_ref, *, tm):
    i = pl.program_id(0)
    last = pl.num_programs(0) - 1
    outs = (kd_ref, ks_ref, vd_ref, vs_ref)

    @pl.when(i == 0)
    def _():
        for j, meta in enumerate((mkd_ref, mks_ref, mvd_ref, mvs_ref)):
            tail_ref[j] = meta[...]

    @pl.when(i < last)
    def _():
        qd, qs, kd, ks, vd, vs = _proj_math(x_ref, g_ref, w_ref, cos_ref, sin_ref)
        qd_ref[...] = qd.astype(_bf16)
        qs_ref[...] = qs.astype(_bf16)
        for j, (b16_ref, val) in enumerate(((kdb_ref, kd), (ksb_ref, ks), (vdb_ref, vd), (vsb_ref, vs))):
            b16_ref[...] = val.astype(_bf16)
            outs[j][:N_META, :] = tail_ref[j]
            outs[j][N_META:, :] = val[:tm - N_META]
            tail_ref[j] = val[tm - N_META:]

    @pl.when(i == last)
    def _():
        for j in range(len(outs)):
            outs[j][:N_META, :] = tail_ref[j]


def _proj_prompt(x, g, w_bf16, cos, sin, meta_kv, tm):
    rows = x.shape[0]
    nblk = rows // tm
    frame_blk = lambda width: pl.BlockSpec((tm, width), lambda i: (jnp.minimum(i, nblk - 1), 0))
    full = lambda a: pl.BlockSpec(a.shape, lambda i: (0, 0))
    f32_out = jax.ShapeDtypeStruct((N_META + rows, DIFF_WIDTH), _f32)
    b16_out = jax.ShapeDtypeStruct((rows, DIFF_WIDTH), _bf16)
    return pl.pallas_call(
        functools.partial(_proj_prompt_kernel, tm=tm),
        grid=(nblk + 1,),
        in_specs=[frame_blk(D_MODEL), full(g), full(w_bf16), frame_blk(LANES), frame_blk(LANES)]
                 + [full(m) for m in meta_kv],
        out_specs=[frame_blk(DIFF_WIDTH)] * 2
                  + [pl.BlockSpec((tm, DIFF_WIDTH), lambda i: (i, 0))] * 4
                  + [frame_blk(DIFF_WIDTH)] * 4,
        out_shape=[b16_out, b16_out, f32_out, f32_out, f32_out, f32_out,
                   b16_out, b16_out, b16_out, b16_out],
        scratch_shapes=[pltpu.VMEM((4, N_META, DIFF_WIDTH), _f32)],
        compiler_params=pltpu.CompilerParams(
            dimension_semantics=("arbitrary",), vmem_limit_bytes=VMEM_LIMIT),
        name="proj_prompt",
    )(x, g, w_bf16, cos, sin, *meta_kv)


def _rope_tables(pos):
    half = HEAD_DIM // 2
    inv_freq = ROPE_THETA ** (-jnp.arange(half, dtype=_f32) / half)
    ang = pos.astype(_f32)[:, None] * inv_freq[None, :]
    cos = jnp.cos(ang)
    sin = jnp.sin(ang)
    reps = LANES // HEAD_DIM
    return (jnp.concatenate([cos, cos] * reps, axis=1),
            jnp.concatenate([-sin, sin] * reps, axis=1))


def _lane_half_masks(shape):
    lane = lax.broadcasted_iota(jnp.int32, shape, 1)
    return lane < HEAD_DIM, lane >= HEAD_DIM


def _lambda(lq1, lk1, lq2, lk2):
    return (jnp.exp(jnp.sum(lq1 * lk1, axis=-1, keepdims=True))
            - jnp.exp(jnp.sum(lq2 * lk2, axis=-1, keepdims=True)) + LAMBDA_INIT)


def _diff_finish(o0, l0, o1, l1, lam, g_head):
    o = o0 / l0 - lam * (o1 / l1)
    return _rms(o, g_head) * (1.0 - LAMBDA_INIT)


def _diff_kernel(q_ref, k_ref, v_ref, km_ref, vm_ref, lq1_ref, lk1_ref, lq2_ref, lk2_ref, gh_ref,
                 o_ref, m_ref, acc_ref, *, bq, bk, heads):
    qi = pl.program_id(1)
    lo, hi = _lane_half_masks((bq, LANES))
    zero = jnp.zeros((bq, LANES), _bf16)
    chains = []
    for a in range(heads):
        sl = slice(a * LANES, (a + 1) * LANES)
        q = q_ref[:, sl]
        chains.append((2 * a, sl, jnp.where(lo, q, zero)))
        chains.append((2 * a + 1, sl, jnp.where(hi, q, zero)))

    def softmax_pv(c, s, v, mask, first):
        if mask is not None:
            s = jnp.where(mask, s, NEG_INF)
        m_cur = jnp.max(s, axis=-1, keepdims=True)
        m_new = jnp.broadcast_to(m_cur, (bq, LANES)) if first else jnp.maximum(m_ref[c], m_cur)
        p = jnp.exp2((s - jnp.tile(m_new, (1, s.shape[1] // LANES))).astype(_bf16))
        pv = _dot(p, jnp.concatenate([v, jnp.ones_like(v)], axis=1))
        if first:
            acc_ref[c] = pv
        else:
            alpha = jnp.exp2(m_ref[c] - m_new)
            acc_ref[c] = jnp.tile(alpha, (1, 2)) * acc_ref[c] + pv
        m_ref[c] = m_new

    def step(k_of, v_of, mask, first):
        scores = {}
        for i in range(len(chains) + SCORE_LEAD):
            if i < len(chains):
                _, sl, qz = chains[i]
                scores[i] = _dot_nt(qz, k_of(sl))
            if i >= SCORE_LEAD:
                c, sl, _ = chains[i - SCORE_LEAD]
                softmax_pv(c, scores.pop(i - SCORE_LEAD), v_of(sl), mask, first)

    mcol = lax.broadcasted_iota(jnp.int32, (bq, META_PAD), 1)
    step(lambda sl: km_ref[:, sl], lambda sl: vm_ref[:, sl], mcol < N_META, True)

    q0 = qi * bq
    n_full = (q0 + CHUNK) // bk

    def tile_at(start, mask):
        step(lambda sl: k_ref[pl.ds(start, bk), sl], lambda sl: v_ref[pl.ds(start, bk), sl], mask, False)

    def full_tile(j, carry):
        tile_at(pl.multiple_of(j * bk, bk), None)
        return carry

    lax.fori_loop(0, n_full, full_tile, 0)

    start = pl.multiple_of(n_full * bk, bk)
    row = lax.broadcasted_iota(jnp.int32, (bq, bk), 0) + q0
    col = lax.broadcasted_iota(jnp.int32, (bq, bk), 1) + start
    tile_at(start, (col >> CHUNK_SHIFT) <= (row >> CHUNK_SHIFT))

    lam = _lambda(lq1_ref[...], lk1_ref[...], lq2_ref[...], lk2_ref[...])
    for a in range(heads):
        acc0, acc1 = acc_ref[2 * a], acc_ref[2 * a + 1]
        out = _diff_finish(acc0[:, :LANES], acc0[:, LANES:], acc1[:, :LANES], acc1[:, LANES:],
                           lam, gh_ref[...])
        o_ref[:, a * LANES:(a + 1) * LANES] = out.astype(_bf16)


def _diff_attn(qd, kdb, vdb, kmeta, vmeta, lq1, lk1, lq2, lk2, g_head, bq=512, bk=512,
               heads=N_HEADS_DIFF):
    n = qd.shape[0]
    width = heads * LANES
    head_blk = lambda rows: pl.BlockSpec((rows, width), lambda h, i: (0, h),
                                         pipeline_mode=pl.Buffered(1))
    small = lambda a: pl.BlockSpec(a.shape, lambda h, i: (0, 0))
    return pl.pallas_call(
        functools.partial(_diff_kernel, bq=bq, bk=bk, heads=heads),
        grid=(N_HEADS_DIFF // heads, n // bq),
        in_specs=[pl.BlockSpec((bq, width), lambda h, i: (i, h)),
                  head_blk(n), head_blk(n), head_blk(META_PAD), head_blk(META_PAD),
                  small(lq1), small(lk1), small(lq2), small(lk2), small(g_head)],
        out_specs=pl.BlockSpec((bq, width), lambda h, i: (i, h)),
        out_shape=jax.ShapeDtypeStruct((n, DIFF_WIDTH), _bf16),
        scratch_shapes=[pltpu.VMEM((2 * heads, bq, LANES), _f32),
                        pltpu.VMEM((2 * heads, bq, 2 * LANES), _f32)],
        compiler_params=pltpu.CompilerParams(
            dimension_semantics=("arbitrary", "arbitrary"), vmem_limit_bytes=VMEM_LIMIT),
        name="diff_attn",
    )(qd, kdb, vdb, kmeta, vmeta, lq1, lk1, lq2, lk2, g_head)


def _suffix_matrix(n):
    j = lax.broadcasted_iota(jnp.int32, (n, n), 0)
    s = lax.broadcasted_iota(jnp.int32, (n, n), 1)
    return jnp.where(j > s, 1.0, 0.0).astype(_bf16)


def _sb_tiles(queries, keys, values, carries, mask, suffix):
    scores = [_dot_nt(q, k) for q, k in zip(queries, keys)]
    log_betas, log_keeps, splits = [], [], []
    for s in scores:
        log_beta = _log_sigmoid(s)
        log_keep = log_beta - s
        if mask is not None:
            log_keep = jnp.where(mask, log_keep, 0.0)
        keep_hi = log_keep.astype(_bf16)
        keep_lo = (log_keep - keep_hi.astype(_f32)).astype(_bf16)
        log_betas.append(log_beta)
        log_keeps.append(log_keep)
        splits.append((keep_hi, keep_lo))
    afters = [_dot(hi, suffix) + _dot(lo, suffix) for hi, lo in splits]
    weights, new_carries = [], []
    for log_beta, log_keep, after, carry in zip(log_betas, log_keeps, afters, carries):
        a = jnp.exp(log_beta + after + carry)
        if mask is not None:
            a = jnp.where(mask, a, 0.0)
        weights.append(a.astype(_bf16))
        new_carries.append(carry + after[:, 0:1] + log_keep[:, 0:1])
    return [_dot(a, v) for a, v in zip(weights, values)], new_carries


def _sb_kernel(q_ref, k_ref, v_ref, km_ref, vm_ref, o_ref, acc_ref, carry_ref, *, bq):
    qi = pl.program_id(0)
    lo, hi = _lane_half_masks((bq, LANES))
    zero = jnp.zeros((bq, LANES), _bf16)
    lanes, queries = [], []
    for p in range(SB_WIDTH // LANES):
        sl = slice(p * LANES, (p + 1) * LANES)
        q = q_ref[:, sl]
        lanes += [sl, sl]
        queries += [jnp.where(lo, q, zero), jnp.where(hi, q, zero)]
    heads = range(N_HEADS_SB)
    suffix = _suffix_matrix(bq)

    def tile(k_of, v_of, mask, suffix, first):
        carries = [jnp.zeros((bq, 1), _f32) if first else carry_ref[c] for c in heads]
        pvs, carries = _sb_tiles(queries, [k_of(sl) for sl in lanes], [v_of(sl) for sl in lanes],
                                 carries, mask, suffix)
        live = carries[0]
        for c in heads:
            acc_ref[c] = pvs[c] if first else acc_ref[c] + pvs[c]
            carry_ref[c] = carries[c]
            live = jnp.maximum(live, carries[c])
        return jnp.max(live)

    def frame_tile(j, mask, first):
        start = pl.multiple_of(j * bq, bq)
        return tile(lambda sl: k_ref[pl.ds(start, bq), sl], lambda sl: v_ref[pl.ds(start, bq), sl],
                    mask, suffix, first)

    row = lax.broadcasted_iota(jnp.int32, (bq, bq), 0)
    col = lax.broadcasted_iota(jnp.int32, (bq, bq), 1)
    live = frame_tile(qi, col < row, True)

    def cond(state):
        j, live = state
        return jnp.logical_and(j >= 0, live > SB_LOG_STOP)

    def body(state):
        j, _ = state
        return j - 1, frame_tile(j, None, False)

    _, live = lax.while_loop(cond, body, (qi - 1, live))

    @pl.when(live > SB_LOG_STOP)
    def _():
        mcol = lax.broadcasted_iota(jnp.int32, (bq, META_PAD), 1)
        tile(lambda sl: km_ref[:, sl], lambda sl: vm_ref[:, sl], mcol < N_META,
             _suffix_matrix(META_PAD), False)

    for p in range(SB_WIDTH // LANES):
        o_ref[:, p * LANES:(p + 1) * LANES] = jnp.where(lo, acc_ref[2 * p], acc_ref[2 * p + 1]).astype(_bf16)


def _sb_attn(qs, ksb, vsb, kmeta, vmeta, bq=128):
    n = qs.shape[0]
    resident = lambda a: pl.BlockSpec(a.shape, lambda i: (0, 0), pipeline_mode=pl.Buffered(1))
    return pl.pallas_call(
        functools.partial(_sb_kernel, bq=bq),
        grid=(n // bq,),
        in_specs=[pl.BlockSpec((bq, SB_WIDTH), lambda i: (i, 0)),
                  resident(ksb), resident(vsb), resident(kmeta), resident(vmeta)],
        out_specs=pl.BlockSpec((bq, SB_WIDTH), lambda i: (i, 0)),
        out_shape=jax.ShapeDtypeStruct((n, SB_WIDTH), _bf16),
        scratch_shapes=[pltpu.VMEM((N_HEADS_SB, bq, LANES), _f32),
                        pltpu.VMEM((N_HEADS_SB, bq, 1), _f32)],
        compiler_params=pltpu.CompilerParams(
            dimension_semantics=("arbitrary",), vmem_limit_bytes=VMEM_LIMIT),
        name="sb_attn",
    )(qs, ksb, vsb, kmeta, vmeta)


def _samp_kernel(qd_ref, qs_ref, kdn_ref, ksn_ref, vdn_ref, vsn_ref,
                 kdm_ref, ksm_ref, vdm_ref, vsm_ref,
                 ckd_ref, cks_ref, cvd_ref, cvs_ref,
                 lq1_ref, lk1_ref, lq2_ref, lk2_ref, gh_ref,
                 od_ref, os_ref, kd_edge, ks_edge, vd_edge, vs_edge, *, ts, past):
    for edge, meta, new in ((kd_edge, kdm_ref, kdn_ref), (ks_edge, ksm_ref, ksn_ref),
                            (vd_edge, vdm_ref, vdn_ref), (vs_edge, vsm_ref, vsn_ref)):
        edge[0:N_META, :] = meta[0:N_META, :]
        edge[N_META:N_META + ts, :] = new[...]
        edge[N_META + ts:, :] = jnp.zeros((LANES - N_META - ts, DIFF_WIDTH), _bf16)

    grp = 2 * ts
    n_cols = past + LANES
    col = lax.broadcasted_iota(jnp.int32, (grp, n_cols), 1)
    ecol = col - past
    pos_k = jnp.where(ecol < 0, col + N_META, jnp.where(ecol < N_META, ecol, ecol + past))
    pos_q = (lax.broadcasted_iota(jnp.int32, (grp, n_cols), 0) & (ts - 1)) + N_META + past
    valid = ecol < N_META + ts
    chunk = lambda p: jnp.where(p < N_META, -1, (p - N_META) >> CHUNK_SHIFT)
    mask_chunk = jnp.logical_and(valid, chunk(pos_k) <= chunk(pos_q))
    mask_strict = jnp.logical_and(valid, pos_k < pos_q)

    lo, hi = _lane_half_masks((ts, LANES))
    zero = jnp.zeros((ts, LANES), _bf16)
    blocks = [slice(j * LANES, (j + 1) * LANES) for j in range(DIFF_WIDTH // LANES)]

    def stacked(q):
        return jnp.concatenate([jnp.where(lo, q, zero), jnp.where(hi, q, zero)], axis=0)

    def scores(q_ref, cache_ref, edge):
        out = []
        for sl in blocks:
            q = stacked(q_ref[:, sl])
            out.append(jnp.concatenate([_dot(q, cache_ref[0, sl, :].astype(_bf16)),
                                        _dot_nt(q, edge[:, sl])], axis=1))
        return out

    diff_scores = scores(qd_ref, ckd_ref, kd_edge)
    sb_scores = scores(qs_ref, cks_ref, ks_edge)

    lam = _lambda(lq1_ref[...], lk1_ref[...], lq2_ref[...], lk2_ref[...])
    probs, sums = [], []
    for s in diff_scores:
        s = jnp.where(mask_chunk, s, NEG_INF)
        p = jnp.exp2(s - jnp.max(s, axis=-1, keepdims=True))
        probs.append(p.astype(_bf16))
        sums.append(jnp.sum(p, axis=-1, keepdims=True))
    for h, (sl, p, l) in enumerate(zip(blocks, probs, sums)):
        v_cache = cvd_ref[0, pl.ds(h, past, stride=N_HEADS_DIFF), :].astype(_bf16)
        pv = _dot(p[:, :past], v_cache) + _dot(p[:, past:], vd_edge[:, sl])
        out = _diff_finish(pv[:ts], l[:ts], pv[ts:], l[ts:], lam, gh_ref[...])
        od_ref[:, sl] = out.astype(_bf16)

    n_tiles = past // LANES
    tiles = [slice(t * LANES, (t + 1) * LANES) for t in range(n_tiles)]
    edge_cols = slice(past, n_cols)
    log_betas, log_keeps, cache_pieces, edge_pieces = [], [], [], []
    for s in sb_scores:
        log_beta = _log_sigmoid(s)
        log_keep = jnp.where(mask_strict, log_beta - s, 0.0)
        keep_hi = log_keep.astype(_bf16)
        keep_lo = (log_keep - keep_hi.astype(_f32)).astype(_bf16)
        log_betas.append(log_beta)
        log_keeps.append(log_keep)
        cache_pieces += [keep_hi[:, t] for t in tiles] + [keep_lo[:, t] for t in tiles]
        edge_pieces += [keep_hi[:, edge_cols], keep_lo[:, edge_cols]]
    after_cache = _dot(jnp.concatenate(cache_pieces, axis=0), _suffix_matrix(LANES))
    ej = lax.broadcasted_iota(jnp.int32, (LANES, LANES), 0)
    es = lax.broadcasted_iota(jnp.int32, (LANES, LANES), 1)
    edge_suffix = jnp.where(jnp.logical_and(ej > es, (ej < N_META) == (es < N_META)), 1.0, 0.0)
    after_edge = _dot(jnp.concatenate(edge_pieces, axis=0), edge_suffix.astype(_bf16))
    is_meta = lax.broadcasted_iota(jnp.int32, (grp, LANES), 1) < N_META
    for j, sl in enumerate(blocks):
        base = j * 2 * n_tiles
        after = [after_cache[(base + t) * grp:(base + t + 1) * grp]
                 + after_cache[(base + n_tiles + t) * grp:(base + n_tiles + t + 1) * grp]
                 for t in range(n_tiles)]
        edge_after = (after_edge[2 * j * grp:(2 * j + 1) * grp]
                      + after_edge[(2 * j + 1) * grp:(2 * j + 2) * grp])
        keep_edge = log_keeps[j][:, edge_cols]
        carry = edge_after[:, N_META:N_META + 1] + keep_edge[:, N_META:N_META + 1]
        shifted = [None] * n_tiles
        for t in reversed(range(n_tiles)):
            shifted[t] = after[t] + carry
            carry = carry + after[t][:, 0:1] + log_keeps[j][:, tiles[t]][:, 0:1]
        shifted.append(edge_after + jnp.where(is_meta, carry, 0.0))
        a = jnp.exp(log_betas[j] + jnp.concatenate(shifted, axis=1))
        a = jnp.where(mask_strict, a, 0.0).astype(_bf16)
        pv = _dot_nt(a[:, :past], cvs_ref[0, sl, :].astype(_bf16)) + _dot(a[:, past:], vs_edge[:, sl])
        os_ref[:, sl] = jnp.where(lo, pv[:ts], pv[ts:]).astype(_bf16)


def _samp_attn(qd, qs, kdn, ksn, vdn, vsn, kdm, ksm, vdm, vsm, ckd_t, cks_t, cvd_rows, cvs_t,
               lq1, lk1, lq2, lk2, g_head, ts):
    nb, _, past = ckd_t.shape
    assert past % LANES == 0 and N_META + ts <= LANES
    row_blk = pl.BlockSpec((ts, DIFF_WIDTH), lambda b: (b, 0))
    per_batch = lambda a: pl.BlockSpec((1,) + a.shape[1:], lambda b: (b, 0, 0))
    small = lambda a: pl.BlockSpec(a.shape, lambda b: (0, 0))
    out = jax.ShapeDtypeStruct((nb * ts, DIFF_WIDTH), _bf16)
    return pl.pallas_call(
        functools.partial(_samp_kernel, ts=ts, past=past),
        grid=(nb,),
        in_specs=[row_blk] * 6 + [small(kdm)] * 4
                 + [per_batch(ckd_t), per_batch(cks_t), per_batch(cvd_rows), per_batch(cvs_t)]
                 + [small(lq1), small(lk1), small(lq2), small(lk2), small(g_head)],
        out_specs=[row_blk, row_blk],
        out_shape=[out, out],
        scratch_shapes=[pltpu.VMEM((LANES, DIFF_WIDTH), _bf16)] * 4,
        compiler_params=pltpu.CompilerParams(
            dimension_semantics=("arbitrary",), vmem_limit_bytes=VMEM_LIMIT),
        name="samp_attn",
    )(qd, qs, kdn, ksn, vdn, vsn, kdm, ksm, vdm, vsm, ckd_t, cks_t, cvd_rows, cvs_t,
      lq1, lk1, lq2, lk2, g_head)


def _mlp_kernel(x_ref, md_ref, ms_ref, wo_ref, gm_ref, wu_ref, wd_ref, gf_ref, o_ref,
                x1_ref, h2_ref, acc_ref):
    f = pl.program_id(1)

    @pl.when(f == 0)
    def _():
        x1 = (x_ref[...] + _dot(md_ref[...], wo_ref[:DIFF_WIDTH, :])
              + _dot(ms_ref[...], wo_ref[DIFF_WIDTH:, :]))
        x1_ref[...] = x1
        h2_ref[...] = _rms(x1, gm_ref[...]).astype(_bf16)
        acc_ref[...] = jnp.zeros(acc_ref.shape, _f32)

    u = jnp.maximum(_dot(h2_ref[...], wu_ref[...]), 0.0)
    acc_ref[...] += _dot((u * u).astype(_bf16), wd_ref[...])

    @pl.when(f == pl.num_programs(1) - 1)
    def _():
        o_ref[...] = _rms(x1_ref[...] + acc_ref[...], gf_ref[...])


def _mlp(x, mixed_d, mixed_s, wo, g_mlp, wu, wd, g_final, tm=512, tf=1024):
    rows = x.shape[0]
    small = lambda a: pl.BlockSpec(a.shape, lambda i, f: (0, 0))
    return pl.pallas_call(
        _mlp_kernel,
        grid=(rows // tm, D_FF // tf),
        in_specs=[pl.BlockSpec((tm, D_MODEL), lambda i, f: (i, 0)),
                  pl.BlockSpec((tm, DIFF_WIDTH), lambda i, f: (i, 0)),
                  pl.BlockSpec((tm, SB_WIDTH), lambda i, f: (i, 0)),
                  small(wo), small(g_mlp),
                  pl.BlockSpec((D_MODEL, tf), lambda i, f: (0, f)),
                  pl.BlockSpec((tf, D_MODEL), lambda i, f: (f, 0)),
                  small(g_final)],
        out_specs=pl.BlockSpec((tm, D_MODEL), lambda i, f: (i, 0)),
        out_shape=jax.ShapeDtypeStruct((rows, D_MODEL), _f32),
        scratch_shapes=[pltpu.VMEM((tm, D_MODEL), _f32), pltpu.VMEM((tm, D_MODEL), _bf16),
                        pltpu.VMEM((tm, D_MODEL), _f32)],
        compiler_params=pltpu.CompilerParams(
            dimension_semantics=("arbitrary", "arbitrary"), vmem_limit_bytes=VMEM_LIMIT),
        name="mlp",
    )(x, mixed_d, mixed_s, wo, g_mlp, wu, wd, g_final)


def kernel(x_prompt, x_sample, cache_diff_k, cache_diff_v, cache_sb_k, cache_sb_v, meta_tokens, g_mix, w_in, lambda_q1, lambda_k1, lambda_q2, lambda_k2, g_diff_head, w_out, g_mlp, w_up, w_down, g_final):
    bp, tp, d = x_prompt.shape
    bs, ts, _ = x_sample.shape
    past = cache_diff_k.shape[2]
    assert bp == 1 and g_mix.shape[0] == 1, "single prompt sequence, depth 1"

    w_in_b = w_in[0].astype(_bf16)
    w_out_b = w_out[0].astype(_bf16)
    w_up_b = w_up[0].astype(_bf16)
    w_down_b = w_down[0].astype(_bf16)
    g_final2 = g_final[None, :]
    lams = (lambda_q1, lambda_k1, lambda_q2, lambda_k2)

    x_ms = jnp.concatenate([meta_tokens.astype(x_sample.dtype), x_sample.reshape(bs * ts, d)], axis=0)
    pos_ms = jnp.concatenate([jnp.arange(N_META), jnp.tile(N_META + past + jnp.arange(ts), bs)])
    cos_s, sin_s = _rope_tables(pos_ms)
    s_out = _proj(x_ms, g_mix, w_in_b, cos_s, sin_s)
    meta_f32 = [a[:N_META] for a in s_out[2:6]]
    samp_f32 = [a[N_META:] for a in s_out[2:6]]
    pad_meta = lambda a: jnp.pad(a[:N_META], ((0, META_PAD - N_META), (0, 0)))
    kdm, ksm, vdm, vsm = [pad_meta(a) for a in s_out[6:10]]
    qd_s, qs_s = s_out[0][N_META:], s_out[1][N_META:]
    kdn, ksn, vdn, vsn = [a[N_META:] for a in s_out[6:10]]

    cos_p, sin_p = _rope_tables(N_META + jnp.arange(tp))
    p_out = _proj_prompt(x_prompt[0], g_mix, w_in_b, cos_p, sin_p, meta_f32, tm=512)
    qd, qs, kd, ks, vd, vs, kdb, ksb, vdb, vsb = p_out

    mixed_d = _diff_attn(qd, kdb, vdb, kdm, vdm, *lams, g_diff_head)
    mixed_s = _sb_attn(qs, ksb, vsb, ksm, vsm)
    y_prompt = _mlp(x_prompt[0], mixed_d, mixed_s, w_out_b, g_mlp, w_up_b, w_down_b, g_final2)

    feat_major = lambda c: jnp.transpose(c[0].reshape(bs, past, DIFF_WIDTH), (0, 2, 1))
    smix_d, smix_s = _samp_attn(qd_s, qs_s, kdn, ksn, vdn, vsn, kdm, ksm, vdm, vsm,
                                feat_major(cache_diff_k), feat_major(cache_sb_k),
                                cache_diff_v[0].reshape(bs, past * N_HEADS_DIFF, 2 * HEAD_DIM),
                                feat_major(cache_sb_v), *lams, g_diff_head, ts)
    y_sample = _mlp(x_sample.reshape(bs * ts, d), smix_d, smix_s, w_out_b, g_mlp, w_up_b, w_down_b,
                    g_final2)

    def prompt_cache(a, tail):
        return a.reshape((1, bp, N_META + tp) + tail)

    def sample_cache(a, tail):
        return a.reshape((1, bs, ts) + tail)

    dk_tail, dv_tail = (N_HEADS_DIFF, 2, HEAD_DIM), (N_HEADS_DIFF, 2 * HEAD_DIM)
    sb_tail = (N_HEADS_SB, HEAD_DIM)
    return (y_prompt[None], y_sample.reshape(bs, ts, d),
            prompt_cache(kd, dk_tail), prompt_cache(vd, dv_tail),
            prompt_cache(ks, sb_tail), prompt_cache(vs, sb_tail),
            sample_cache(samp_f32[0], dk_tail), sample_cache(samp_f32[2], dv_tail),
            sample_cache(samp_f32[1], sb_tail), sample_cache(samp_f32[3], sb_tail))
```

```python
import functools
import math

import jax
import jax.numpy as jnp
from jax import lax
from jax.experimental import pallas as pl
from jax.experimental.pallas import tpu as pltpu

D_MODEL = 1024
CHUNK = 64
CHUNK_SHIFT = 6
N_META = 16
HEAD_DIM = 64
N_HEADS_DIFF = 4
N_HEADS_SB = 8
DIFF_WIDTH = N_HEADS_DIFF * 2 * HEAD_DIM
SB_WIDTH = N_HEADS_SB * HEAD_DIM
MIX_WIDTH = DIFF_WIDTH + SB_WIDTH
D_IN = 3 * MIX_WIDTH
D_FF = 4 * D_MODEL
ROPE_THETA = 10000.0
NORM_EPS = 1e-6
NEG_INF = -1e30
LAMBDA_INIT = 0.8 - 0.6 * math.exp(-0.3 * 0)
LOG2_E = math.log2(math.e)

LANES = 128
META_PAD = 128
SB_LOG_STOP = -40.0
SCORE_LEAD = 3

VMEM_LIMIT = 56 * 1024 * 1024

_f32 = jnp.float32
_bf16 = jnp.bfloat16


def _dot(a, b):
    return jnp.dot(a, b, preferred_element_type=_f32)


def _dot_nt(a, b):
    return lax.dot_general(a, b, (((1,), (1,)), ((), ())), preferred_element_type=_f32)


def _rms(x, g):
    return x * lax.rsqrt(jnp.mean(x * x, axis=-1, keepdims=True) + NORM_EPS) * g


def _log_sigmoid(s):
    return jnp.minimum(s, 0.0) - jnp.log(1.0 + jnp.exp(-jnp.abs(s)))


def _first_half_lanes(shape):
    lane = lax.broadcasted_iota(jnp.int32, shape, 1)
    return (lane & (HEAD_DIM - 1)) < (HEAD_DIM // 2)


def _proj_math(x_ref, g_ref, w_ref, cos, sin):
    h = _rms(x_ref[...], g_ref[...]).astype(_bf16)
    first_half = _first_half_lanes(cos.shape)

    def rope(z):
        parts = []
        for j in range(z.shape[1] // LANES):
            zj = z[:, j * LANES:(j + 1) * LANES]
            swapped = jnp.where(first_half,
                                pltpu.roll(zj, LANES - HEAD_DIM // 2, 1),
                                pltpu.roll(zj, HEAD_DIM // 2, 1))
            parts.append(zj * cos + swapped * sin)
        return jnp.concatenate(parts, axis=1)

    def cols(c):
        return _dot(h, w_ref[:, c * DIFF_WIDTH:(c + 1) * DIFF_WIDTH])

    scale = HEAD_DIM ** -0.5
    return (rope(cols(0)) * (scale * LOG2_E), cols(1) * scale, rope(cols(2)), cols(3), cols(4), cols(5))


def _proj_kernel(x_ref, g_ref, w_ref, cos_ref, sin_ref,
                 qd_ref, qs_ref, kd_ref, ks_ref, vd_ref, vs_ref,
                 kdb_ref, ksb_ref, vdb_ref, vsb_ref):
    qd, qs, kd, ks, vd, vs = _proj_math(x_ref, g_ref, w_ref, cos_ref[...], sin_ref[...])
    qd_ref[...] = qd.astype(_bf16)
    qs_ref[...] = qs.astype(_bf16)
    for f32_ref, b16_ref, val in ((kd_ref, kdb_ref, kd), (ks_ref, ksb_ref, ks),
                                  (vd_ref, vdb_ref, vd), (vs_ref, vsb_ref, vs)):
        f32_ref[...] = val
        b16_ref[...] = val.astype(_bf16)


def _proj(x, g, w_bf16, cos, sin):
    rows = x.shape[0]
    full = lambda a: pl.BlockSpec(a.shape, lambda i: (0, 0))
    blk = lambda width: pl.BlockSpec((rows, width), lambda i: (0, 0))
    f32_out = jax.ShapeDtypeStruct((rows, DIFF_WIDTH), _f32)
    b16_out = jax.ShapeDtypeStruct((rows, DIFF_WIDTH), _bf16)
    return pl.pallas_call(
        _proj_kernel,
        grid=(1,),
        in_specs=[full(x), full(g), full(w_bf16), full(cos), full(sin)],
        out_specs=[blk(DIFF_WIDTH)] * 10,
        out_shape=[b16_out, b16_out, f32_out, f32_out, f32_out, f32_out,
                   b16_out, b16_out, b16_out, b16_out],
        compiler_params=pltpu.CompilerParams(
            dimension_semantics=("arbitrary",), vmem_limit_bytes=VMEM_LIMIT),
        name="proj",
    )(x, g, w_bf16, cos, sin)


def _proj_prompt_kernel(x_ref, g_ref, w_ref, cos_row_ref, sin_row_ref, cos_blk_ref, sin_blk_ref,
                        mkd_ref, mks_ref, mvd_ref, mvs_ref,
                        qd_ref, qs_ref, kd_ref, ks_ref, vd_ref, vs_ref,
                        kdb_ref, ksb_ref, vdb_ref, vsb_ref, tail_ref, *, tm):
    i = pl.program_id(0)
    last = pl.num_programs(0) - 1
    row_outs = ((0, kd_ref), (1, ks_ref), (3, vs_ref))

    def put_vd(token0, rows):
        for h in range(N_HEADS_DIFF):
            vd_ref[pl.ds(token0 * N_HEADS_DIFF + h, rows.shape[0], stride=N_HEADS_DIFF), :] = (
                rows[:, h * LANES:(h + 1) * LANES])

    @pl.when(i == 0)
    def _():
        for j, meta in enumerate((mkd_ref, mks_ref, mvd_ref, mvs_ref)):
            tail_ref[j] = meta[...]

    @pl.when(i < last)
    def _():
        cos_r, sin_r, cos_b, sin_b = cos_row_ref[...], sin_row_ref[...], cos_blk_ref[0], sin_blk_ref[0]
        cos = cos_b * cos_r - sin_b * sin_r
        sin = sin_b * cos_r + cos_b * sin_r
        sin = jnp.where(_first_half_lanes(sin.shape), -sin, sin)
        qd, qs, kd, ks, vd, vs = _proj_math(x_ref, g_ref, w_ref, cos, sin)
        qd_ref[...] = qd.astype(_bf16)
        qs_ref[...] = qs.astype(_bf16)
        for b16_ref, val in ((kdb_ref, kd), (ksb_ref, ks), (vdb_ref, vd), (vsb_ref, vs)):
            b16_ref[...] = val.astype(_bf16)
        for (j, out), val in zip(row_outs, (kd, ks, vs)):
            out[:N_META, :] = tail_ref[j]
            out[N_META:, :] = val[:tm - N_META]
            tail_ref[j] = val[tm - N_META:]
        put_vd(0, tail_ref[2])
        put_vd(N_META, vd[:tm - N_META])
        tail_ref[2] = vd[tm - N_META:]

    @pl.when(i == last)
    def _():
        for j, out in row_outs:
            out[:N_META, :] = tail_ref[j]
        put_vd(0, tail_ref[2])


def _proj_prompt(x, g, w_bf16, pos0, meta_kv, tm):
    rows = x.shape[0]
    nblk = rows // tm
    cos_row, sin_row = _angle_tables(jnp.arange(tm))
    cos_blk, sin_blk = [t[:, None, :] for t in _angle_tables(pos0 + tm * jnp.arange(nblk))]
    frame_blk = lambda width: pl.BlockSpec((tm, width), lambda i: (jnp.minimum(i, nblk - 1), 0))
    blk_angle = pl.BlockSpec((1, 1, LANES), lambda i: (jnp.minimum(i, nblk - 1), 0, 0))
    full = lambda a: pl.BlockSpec(a.shape, lambda i: (0, 0))
    f32_out = jax.ShapeDtypeStruct((N_META + rows, DIFF_WIDTH), _f32)
    vd_out = jax.ShapeDtypeStruct(((N_META + rows) * N_HEADS_DIFF, LANES), _f32)
    b16_out = jax.ShapeDtypeStruct((rows, DIFF_WIDTH), _bf16)
    stream_blk = pl.BlockSpec((tm, DIFF_WIDTH), lambda i: (i, 0))
    return pl.pallas_call(
        functools.partial(_proj_prompt_kernel, tm=tm),
        grid=(nblk + 1,),
        in_specs=[frame_blk(D_MODEL), full(g), full(w_bf16), full(cos_row), full(sin_row),
                  blk_angle, blk_angle] + [full(m) for m in meta_kv],
        out_specs=[frame_blk(DIFF_WIDTH)] * 2
                  + [stream_blk, stream_blk,
                     pl.BlockSpec((tm * N_HEADS_DIFF, LANES), lambda i: (i, 0)), stream_blk]
                  + [frame_blk(DIFF_WIDTH)] * 4,
        out_shape=[b16_out, b16_out, f32_out, f32_out, vd_out, f32_out,
                   b16_out, b16_out, b16_out, b16_out],
        scratch_shapes=[pltpu.VMEM((4, N_META, DIFF_WIDTH), _f32)],
        compiler_params=pltpu.CompilerParams(
            dimension_semantics=("arbitrary",), vmem_limit_bytes=VMEM_LIMIT),
        name="proj_prompt",
    )(x, g, w_bf16, cos_row, sin_row, cos_blk, sin_blk, *meta_kv)


def _angle_tables(pos):
    half = HEAD_DIM // 2
    inv_freq = ROPE_THETA ** (-jnp.arange(half, dtype=_f32) / half)
    ang = pos.astype(_f32)[:, None] * inv_freq[None, :]
    reps = LANES // half
    return jnp.tile(jnp.cos(ang), (1, reps)), jnp.tile(jnp.sin(ang), (1, reps))


def _rope_tables(pos):
    cos, sin = _angle_tables(pos)
    return cos, jnp.where(_first_half_lanes(sin.shape), -sin, sin)


def _lane_half_masks(shape):
    lane = lax.broadcasted_iota(jnp.int32, shape, 1)
    return lane < HEAD_DIM, lane >= HEAD_DIM


def _lambda(lq1, lk1, lq2, lk2):
    return (jnp.exp(jnp.sum(lq1 * lk1, axis=-1, keepdims=True))
            - jnp.exp(jnp.sum(lq2 * lk2, axis=-1, keepdims=True)) + LAMBDA_INIT)


def _diff_finish(o0, l0, o1, l1, lam, g_head):
    o = o0 / l0 - lam * (o1 / l1)
    return _rms(o, g_head) * (1.0 - LAMBDA_INIT)


def _diff_kernel(q_ref, k_ref, v_ref, km_ref, vm_ref, lq1_ref, lk1_ref, lq2_ref, lk2_ref, gh_ref,
                 o_ref, m_ref, acc_ref, *, bq, bk, heads):
    qi = pl.program_id(1)
    lo, hi = _lane_half_masks((bq, LANES))
    zero = jnp.zeros((bq, LANES), _bf16)
    chains = []
    for a in range(heads):
        sl = slice(a * LANES, (a + 1) * LANES)
        q = q_ref[:, sl]
        chains.append((2 * a, sl, jnp.where(lo, q, zero)))
        chains.append((2 * a + 1, sl, jnp.where(hi, q, zero)))

    def softmax_pv(c, s, v, mask, first):
        if mask is not None:
            s = jnp.where(mask, s, NEG_INF)
        m_cur = jnp.max(s, axis=-1, keepdims=True)
        m_new = jnp.broadcast_to(m_cur, (bq, LANES)) if first else jnp.maximum(m_ref[c], m_cur)
        p = jnp.exp2((s - jnp.tile(m_new, (1, s.shape[1] // LANES))).astype(_bf16))
        pv = _dot(p, jnp.concatenate([v, jnp.ones_like(v)], axis=1))
        if first:
            acc_ref[c] = pv
        else:
            alpha = jnp.exp2(m_ref[c] - m_new)
            acc_ref[c] = jnp.tile(alpha, (1, 2)) * acc_ref[c] + pv
        m_ref[c] = m_new

    def step(k_of, v_of, mask, first):
        scores = {}
        for i in range(len(chains) + SCORE_LEAD):
            if i < len(chains):
                _, sl, qz = chains[i]
                scores[i] = _dot_nt(qz, k_of(sl))
            if i >= SCORE_LEAD:
                c, sl, _ = chains[i - SCORE_LEAD]
                softmax_pv(c, scores.pop(i - SCORE_LEAD), v_of(sl), mask, first)

    q0 = qi * bq
    n_full = (q0 + CHUNK) // bk

    start = pl.multiple_of(n_full * bk, bk)
    row = lax.broadcasted_iota(jnp.int32, (bq, bk + META_PAD), 0) + q0
    col = lax.broadcasted_iota(jnp.int32, (bq, bk + META_PAD), 1)
    key_chunk = jnp.where(col < bk, (col + start) >> CHUNK_SHIFT,
                          jnp.where(col < bk + N_META, -1, jnp.iinfo(jnp.int32).max))
    mask = key_chunk <= (row >> CHUNK_SHIFT)
    step(lambda sl: jnp.concatenate([k_ref[pl.ds(start, bk), sl], km_ref[:, sl]], axis=0),
         lambda sl: jnp.concatenate([v_ref[pl.ds(start, bk), sl], vm_ref[:, sl]], axis=0), mask, True)

    def full_tile(j, carry):
        start = pl.multiple_of(j * bk, bk)
        step(lambda sl: k_ref[pl.ds(start, bk), sl], lambda sl: v_ref[pl.ds(start, bk), sl], None, False)
        return carry

    lax.fori_loop(0, n_full, full_tile, 0)

    lam = _lambda(lq1_ref[...], lk1_ref[...], lq2_ref[...], lk2_ref[...])
    for a in range(heads):
        acc0, acc1 = acc_ref[2 * a], acc_ref[2 * a + 1]
        out = _diff_finish(acc0[:, :LANES], acc0[:, LANES:], acc1[:, :LANES], acc1[:, LANES:],
                           lam, gh_ref[...])
        o_ref[:, a * LANES:(a + 1) * LANES] = out.astype(_bf16)


def _diff_attn(qd, kdb, vdb, kmeta, vmeta, lq1, lk1, lq2, lk2, g_head, bq=512, bk=512,
               heads=N_HEADS_DIFF):
    n = qd.shape[0]
    width = heads * LANES
    head_blk = lambda rows: pl.BlockSpec((rows, width), lambda h, i: (0, h),
                                         pipeline_mode=pl.Buffered(1))
    small = lambda a: pl.BlockSpec(a.shape, lambda h, i: (0, 0))
    return pl.pallas_call(
        functools.partial(_diff_kernel, bq=bq, bk=bk, heads=heads),
        grid=(N_HEADS_DIFF // heads, n // bq),
        in_specs=[pl.BlockSpec((bq, width), lambda h, i: (i, h)),
                  head_blk(n), head_blk(n), head_blk(META_PAD), head_blk(META_PAD),
                  small(lq1), small(lk1), small(lq2), small(lk2), small(g_head)],
        out_specs=pl.BlockSpec((bq, width), lambda h, i: (i, h)),
        out_shape=jax.ShapeDtypeStruct((n, DIFF_WIDTH), _bf16),
        scratch_shapes=[pltpu.VMEM((2 * heads, bq, LANES), _f32),
                        pltpu.VMEM((2 * heads, bq, 2 * LANES), _f32)],
        compiler_params=pltpu.CompilerParams(
            dimension_semantics=("arbitrary", "arbitrary"), vmem_limit_bytes=VMEM_LIMIT),
        name="diff_attn",
    )(qd, kdb, vdb, kmeta, vmeta, lq1, lk1, lq2, lk2, g_head)


def _suffix_matrix(n):
    j = lax.broadcasted_iota(jnp.int32, (n, n), 0)
    s = lax.broadcasted_iota(jnp.int32, (n, n), 1)
    return jnp.where(j > s, 1.0, 0.0).astype(_bf16)


def _sb_tiles(queries, keys, values, carries, mask, suffix):
    scores = [_dot_nt(q, k) for q, k in zip(queries, keys)]
    log_betas, log_keeps, splits = [], [], []
    for s in scores:
        log_beta = _log_sigmoid(s)
        log_keep = log_beta - s
        if mask is not None:
            log_keep = jnp.where(mask, log_keep, 0.0)
        keep_hi = log_keep.astype(_bf16)
        keep_lo = (log_keep - keep_hi.astype(_f32)).astype(_bf16)
        log_betas.append(log_beta)
        log_keeps.append(log_keep)
        splits.append((keep_hi, keep_lo))
    afters = [_dot(hi, suffix) + _dot(lo, suffix) for hi, lo in splits]
    weights, new_carries = [], []
    for log_beta, log_keep, after, carry in zip(log_betas, log_keeps, afters, carries):
        a = jnp.exp(log_beta + after + carry)
        if mask is not None:
            a = jnp.where(mask, a, 0.0)
        weights.append(a.astype(_bf16))
        new_carries.append(carry + after[:, 0:1] + log_keep[:, 0:1])
    return [_dot(a, v) for a, v in zip(weights, values)], new_carries


def _sb_kernel(q_ref, k_ref, v_ref, km_ref, vm_ref, o_ref, acc_ref, carry_ref, live_ref, *, bq):
    qi = pl.program_id(0)
    lo, hi = _lane_half_masks((bq, LANES))
    zero = jnp.zeros((bq, LANES), _bf16)
    lanes, queries = [], []
    for p in range(SB_WIDTH // LANES):
        sl = slice(p * LANES, (p + 1) * LANES)
        q = q_ref[:, sl]
        lanes += [sl, sl]
        queries += [jnp.where(lo, q, zero), jnp.where(hi, q, zero)]
    heads = range(N_HEADS_SB)
    bk = 2 * bq

    def tile(k_of, v_of, mask, suffix, first):
        carries = [jnp.zeros((bq, 1), _f32) if first else carry_ref[c] for c in heads]
        pvs, carries = _sb_tiles(queries, [k_of(sl) for sl in lanes], [v_of(sl) for sl in lanes],
                                 carries, mask, suffix)
        live = carries[0]
        for c in heads:
            acc_ref[c] = pvs[c] if first else acc_ref[c] + pvs[c]
            carry_ref[c] = carries[c]
            live = jnp.maximum(live, carries[c])
        return jnp.max(live)

    def frame_tile(start, width, mask, first):
        start = pl.multiple_of(start, bq)
        return tile(lambda sl: k_ref[pl.ds(start, width), sl], lambda sl: v_ref[pl.ds(start, width), sl],
                    mask, _suffix_matrix(width), first)

    q0 = qi * bq
    start0 = jnp.maximum(q0 - bq, 0)
    row = lax.broadcasted_iota(jnp.int32, (bq, bk), 0) + q0
    col = lax.broadcasted_iota(jnp.int32, (bq, bk), 1) + start0
    live = frame_tile(start0, bk, col < row, True)

    def cond(state):
        start, live = state
        return jnp.logical_and(start >= bk, live > SB_LOG_STOP)

    def body(state):
        start, _ = state
        return start - bk, frame_tile(start - bk, bk, None, False)

    start, live = lax.while_loop(cond, body, (start0, live))
    live_ref[0] = live

    @pl.when(jnp.logical_and(start == bq, live > SB_LOG_STOP))
    def _():
        live_ref[0] = frame_tile(0, bq, None, False)

    @pl.when(jnp.logical_and(start < bk, live_ref[0] > SB_LOG_STOP))
    def _():
        mcol = lax.broadcasted_iota(jnp.int32, (bq, META_PAD), 1)
        tile(lambda sl: km_ref[:, sl], lambda sl: vm_ref[:, sl], mcol < N_META,
             _suffix_matrix(META_PAD), False)

    for p in range(SB_WIDTH // LANES):
        o_ref[:, p * LANES:(p + 1) * LANES] = jnp.where(lo, acc_ref[2 * p], acc_ref[2 * p + 1]).astype(_bf16)


def _sb_attn(qs, ksb, vsb, kmeta, vmeta, bq=128):
    n = qs.shape[0]
    resident = lambda a: pl.BlockSpec(a.shape, lambda i: (0, 0), pipeline_mode=pl.Buffered(1))
    return pl.pallas_call(
        functools.partial(_sb_kernel, bq=bq),
        grid=(n // bq,),
        in_specs=[pl.BlockSpec((bq, SB_WIDTH), lambda i: (i, 0)),
                  resident(ksb), resident(vsb), resident(kmeta), resident(vmeta)],
        out_specs=pl.BlockSpec((bq, SB_WIDTH), lambda i: (i, 0)),
        out_shape=jax.ShapeDtypeStruct((n, SB_WIDTH), _bf16),
        scratch_shapes=[pltpu.VMEM((N_HEADS_SB, bq, LANES), _f32),
                        pltpu.VMEM((N_HEADS_SB, bq, 1), _f32),
                        pltpu.SMEM((1,), _f32)],
        compiler_params=pltpu.CompilerParams(
            dimension_semantics=("arbitrary",), vmem_limit_bytes=VMEM_LIMIT),
        name="sb_attn",
    )(qs, ksb, vsb, kmeta, vmeta)


def _samp_kernel(qd_ref, qs_ref, kdn_ref, ksn_ref, vdn_ref, vsn_ref,
                 kdm_ref, ksm_ref, vdm_ref, vsm_ref,
                 ckd_ref, cks_ref, cvd_ref, cvs_ref,
                 lq1_ref, lk1_ref, lq2_ref, lk2_ref, gh_ref,
                 od_ref, os_ref, kd_edge, ks_edge, vd_edge, vs_edge, *, ts, past):
    for edge, meta, new in ((kd_edge, kdm_ref, kdn_ref), (ks_edge, ksm_ref, ksn_ref),
                            (vd_edge, vdm_ref, vdn_ref), (vs_edge, vsm_ref, vsn_ref)):
        edge[0:N_META, :] = meta[0:N_META, :]
        edge[N_META:N_META + ts, :] = new[...]
        edge[N_META + ts:, :] = jnp.zeros((LANES - N_META - ts, DIFF_WIDTH), _bf16)

    grp = 2 * ts
    n_cols = past + LANES
    col = lax.broadcasted_iota(jnp.int32, (grp, n_cols), 1)
    ecol = col - past
    pos_k = jnp.where(ecol < 0, col + N_META, jnp.where(ecol < N_META, ecol, ecol + past))
    pos_q = (lax.broadcasted_iota(jnp.int32, (grp, n_cols), 0) & (ts - 1)) + N_META + past
    valid = ecol < N_META + ts
    chunk = lambda p: jnp.where(p < N_META, -1, (p - N_META) >> CHUNK_SHIFT)
    mask_chunk = jnp.logical_and(valid, chunk(pos_k) <= chunk(pos_q))
    mask_strict = jnp.logical_and(valid, pos_k < pos_q)

    lo, hi = _lane_half_masks((ts, LANES))
    zero = jnp.zeros((ts, LANES), _bf16)
    blocks = [slice(j * LANES, (j + 1) * LANES) for j in range(DIFF_WIDTH // LANES)]

    def stacked(q):
        return jnp.concatenate([jnp.where(lo, q, zero), jnp.where(hi, q, zero)], axis=0)

    def scores(q_ref, cache_ref, edge):
        out = []
        for sl in blocks:
            q = stacked(q_ref[:, sl])
            out.append(jnp.concatenate([_dot(q, cache_ref[0, sl, :].astype(_bf16)),
                                        _dot_nt(q, edge[:, sl])], axis=1))
        return out

    diff_scores = scores(qd_ref, ckd_ref, kd_edge)
    sb_scores = scores(qs_ref, cks_ref, ks_edge)

    lam = _lambda(lq1_ref[...], lk1_ref[...], lq2_ref[...], lk2_ref[...])
    probs, sums = [], []
    for s in diff_scores:
        s = jnp.where(mask_chunk, s, NEG_INF)
        p = jnp.exp2(s - jnp.max(s, axis=-1, keepdims=True))
        probs.append(p.astype(_bf16))
        sums.append(jnp.sum(p, axis=-1, keepdims=True))
    for h, (sl, p, l) in enumerate(zip(blocks, probs, sums)):
        v_cache = cvd_ref[0, pl.ds(h, past, stride=N_HEADS_DIFF), :].astype(_bf16)
        pv = _dot(p[:, :past], v_cache) + _dot(p[:, past:], vd_edge[:, sl])
        out = _diff_finish(pv[:ts], l[:ts], pv[ts:], l[ts:], lam, gh_ref[...])
        od_ref[:, sl] = out.astype(_bf16)

    n_tiles = past // LANES
    tiles = [slice(t * LANES, (t + 1) * LANES) for t in range(n_tiles)]
    edge_cols = slice(past, n_cols)
    log_betas, log_keeps, cache_pieces, edge_pieces = [], [], [], []
    for s in sb_scores:
        log_beta = _log_sigmoid(s)
        log_keep = jnp.where(mask_strict, log_beta - s, 0.0)
        keep_hi = log_keep.astype(_bf16)
        keep_lo = (log_keep - keep_hi.astype(_f32)).astype(_bf16)
        log_betas.append(log_beta)
        log_keeps.append(log_keep)
        cache_pieces += [keep_hi[:, t] for t in tiles] + [keep_lo[:, t] for t in tiles]
        edge_pieces += [keep_hi[:, edge_cols], keep_lo[:, edge_cols]]
    after_cache = _dot(jnp.concatenate(cache_pieces, axis=0), _suffix_matrix(LANES))
    ej = lax.broadcasted_iota(jnp.int32, (LANES, LANES), 0)
    es = lax.broadcasted_iota(jnp.int32, (LANES, LANES), 1)
    edge_suffix = jnp.where(jnp.logical_and(ej > es, (ej < N_META) == (es < N_META)), 1.0, 0.0)
    after_edge = _dot(jnp.concatenate(edge_pieces, axis=0), edge_suffix.astype(_bf16))
    is_meta = lax.broadcasted_iota(jnp.int32, (grp, LANES), 1) < N_META
    for j, sl in enumerate(blocks):
        base = j * 2 * n_tiles
        after = [after_cache[(base + t) * grp:(base + t + 1) * grp]
                 + after_cache[(base + n_tiles + t) * grp:(base + n_tiles + t + 1) * grp]
                 for t in range(n_tiles)]
        edge_after = (after_edge[2 * j * grp:(2 * j + 1) * grp]
                      + after_edge[(2 * j + 1) * grp:(2 * j + 2) * grp])
        keep_edge = log_keeps[j][:, edge_cols]
        carry = edge_after[:, N_META:N_META + 1] + keep_edge[:, N_META:N_META + 1]
        shifted = [None] * n_tiles
        for t in reversed(range(n_tiles)):
            shifted[t] = after[t] + carry
            carry = carry + after[t][:, 0:1] + log_keeps[j][:, tiles[t]][:, 0:1]
        shifted.append(edge_after + jnp.where(is_meta, carry, 0.0))
        a = jnp.exp(log_betas[j] + jnp.concatenate(shifted, axis=1))
        a = jnp.where(mask_strict, a, 0.0).astype(_bf16)
        pv = _dot_nt(a[:, :past], cvs_ref[0, sl, :].astype(_bf16)) + _dot(a[:, past:], vs_edge[:, sl])
        os_ref[:, sl] = jnp.where(lo, pv[:ts], pv[ts:]).astype(_bf16)


def _samp_attn(qd, qs, kdn, ksn, vdn, vsn, kdm, ksm, vdm, vsm, ckd_t, cks_t, cvd_rows, cvs_t,
               lq1, lk1, lq2, lk2, g_head, ts):
    nb, _, past = ckd_t.shape
    assert past % LANES == 0 and N_META + ts <= LANES
    row_blk = pl.BlockSpec((ts, DIFF_WIDTH), lambda b: (b, 0))
    per_batch = lambda a: pl.BlockSpec((1,) + a.shape[1:], lambda b: (b, 0, 0))
    small = lambda a: pl.BlockSpec(a.shape, lambda b: (0, 0))
    out = jax.ShapeDtypeStruct((nb * ts, DIFF_WIDTH), _bf16)
    return pl.pallas_call(
        functools.partial(_samp_kernel, ts=ts, past=past),
        grid=(nb,),
        in_specs=[row_blk] * 6 + [small(kdm)] * 4
                 + [per_batch(ckd_t), per_batch(cks_t), per_batch(cvd_rows), per_batch(cvs_t)]
                 + [small(lq1), small(lk1), small(lq2), small(lk2), small(g_head)],
        out_specs=[row_blk, row_blk],
        out_shape=[out, out],
        scratch_shapes=[pltpu.VMEM((LANES, DIFF_WIDTH), _bf16)] * 4,
        compiler_params=pltpu.CompilerParams(
            dimension_semantics=("arbitrary",), vmem_limit_bytes=VMEM_LIMIT),
        name="samp_attn",
    )(qd, qs, kdn, ksn, vdn, vsn, kdm, ksm, vdm, vsm, ckd_t, cks_t, cvd_rows, cvs_t,
      lq1, lk1, lq2, lk2, g_head)


def _mlp_kernel(x_ref, md_ref, ms_ref, wo_ref, gm_ref, wu_ref, wd_ref, gf_ref, o_ref, h2_ref, *, tf):
    x1 = (x_ref[...] + _dot(md_ref[...], wo_ref[:DIFF_WIDTH, :])
          + _dot(ms_ref[...], wo_ref[DIFF_WIDTH:, :]))
    h2_ref[...] = _rms(x1, gm_ref[...]).astype(_bf16)
    o_ref[...] = x1
    for f in range(D_FF // tf):
        u = jnp.maximum(_dot(h2_ref[...], wu_ref[:, f * tf:(f + 1) * tf]), 0.0)
        o_ref[...] += _dot((u * u).astype(_bf16), wd_ref[f * tf:(f + 1) * tf, :])
    o_ref[...] = _rms(o_ref[...], gf_ref[...])


def _mlp(x, mixed_d, mixed_s, wo, g_mlp, wu, wd, g_final, tm=512, tf=1024):
    rows = x.shape[0]
    resident = lambda a: pl.BlockSpec(a.shape, lambda i: (0, 0), pipeline_mode=pl.Buffered(1))
    small = lambda a: pl.BlockSpec(a.shape, lambda i: (0, 0))
    row_blk = lambda width: pl.BlockSpec((tm, width), lambda i: (i, 0))
    return pl.pallas_call(
        functools.partial(_mlp_kernel, tf=tf),
        grid=(rows // tm,),
        in_specs=[row_blk(D_MODEL), row_blk(DIFF_WIDTH), row_blk(SB_WIDTH),
                  resident(wo), small(g_mlp), resident(wu), resident(wd), small(g_final)],
        out_specs=row_blk(D_MODEL),
        out_shape=jax.ShapeDtypeStruct((rows, D_MODEL), _f32),
        scratch_shapes=[pltpu.VMEM((tm, D_MODEL), _bf16)],
        compiler_params=pltpu.CompilerParams(
            dimension_semantics=("arbitrary",), vmem_limit_bytes=VMEM_LIMIT),
        name="mlp",
    )(x, mixed_d, mixed_s, wo, g_mlp, wu, wd, g_final)


def kernel(x_prompt, x_sample, cache_diff_k, cache_diff_v, cache_sb_k, cache_sb_v, meta_tokens, g_mix, w_in, lambda_q1, lambda_k1, lambda_q2, lambda_k2, g_diff_head, w_out, g_mlp, w_up, w_down, g_final):
    bp, tp, d = x_prompt.shape
    bs, ts, _ = x_sample.shape
    past = cache_diff_k.shape[2]
    assert bp == 1 and g_mix.shape[0] == 1, "single prompt sequence, depth 1"

    w_in_b = w_in[0].astype(_bf16)
    w_out_b = w_out[0].astype(_bf16)
    w_up_b = w_up[0].astype(_bf16)
    w_down_b = w_down[0].astype(_bf16)
    g_final2 = g_final[None, :]
    lams = (lambda_q1, lambda_k1, lambda_q2, lambda_k2)

    x_ms = jnp.concatenate([meta_tokens.astype(x_sample.dtype), x_sample.reshape(bs * ts, d)], axis=0)
    pos_ms = jnp.concatenate([jnp.arange(N_META), jnp.tile(N_META + past + jnp.arange(ts), bs)])
    cos_s, sin_s = _rope_tables(pos_ms)
    s_out = _proj(x_ms, g_mix, w_in_b, cos_s, sin_s)
    meta_f32 = [a[:N_META] for a in s_out[2:6]]
    samp_f32 = [a[N_META:] for a in s_out[2:6]]
    pad_meta = lambda a: jnp.pad(a[:N_META], ((0, META_PAD - N_META), (0, 0)))
    kdm, ksm, vdm, vsm = [pad_meta(a) for a in s_out[6:10]]
    qd_s, qs_s = s_out[0][N_META:], s_out[1][N_META:]
    kdn, ksn, vdn, vsn = [a[N_META:] for a in s_out[6:10]]

    p_out = _proj_prompt(x_prompt[0], g_mix, w_in_b, N_META, meta_f32, tm=512)
    qd, qs, kd, ks, vd, vs, kdb, ksb, vdb, vsb = p_out

    mixed_d = _diff_attn(qd, kdb, vdb, kdm, vdm, *lams, g_diff_head)
    mixed_s = _sb_attn(qs, ksb, vsb, ksm, vsm)
    y_prompt = _mlp(x_prompt[0], mixed_d, mixed_s, w_out_b, g_mlp, w_up_b, w_down_b, g_final2)

    feat_major = lambda c: jnp.transpose(c[0].reshape(bs, past, DIFF_WIDTH), (0, 2, 1))
    smix_d, smix_s = _samp_attn(qd_s, qs_s, kdn, ksn, vdn, vsn, kdm, ksm, vdm, vsm,
                                feat_major(cache_diff_k), feat_major(cache_sb_k),
                                cache_diff_v[0].reshape(bs, past * N_HEADS_DIFF, 2 * HEAD_DIM),
                                feat_major(cache_sb_v), *lams, g_diff_head, ts)
    y_sample = _mlp(x_sample.reshape(bs * ts, d), smix_d, smix_s, w_out_b, g_mlp, w_up_b, w_down_b,
                    g_final2)

    def prompt_cache(a, tail):
        return a.reshape((1, bp, N_META + tp) + tail)

    def sample_cache(a, tail):
        return a.reshape((1, bs, ts) + tail)

    dk_tail, dv_tail = (N_HEADS_DIFF, 2, HEAD_DIM), (N_HEADS_DIFF, 2 * HEAD_DIM)
    sb_tail = (N_HEADS_SB, HEAD_DIM)
    return (y_prompt[None], y_sample.reshape(bs, ts, d),
            prompt_cache(kd, dk_tail), prompt_cache(vd, dv_tail),
            prompt_cache(ks, sb_tail), prompt_cache(vs, sb_tail),
            sample_cache(samp_f32[0], dk_tail), sample_cache(samp_f32[2], dv_tail),
            sample_cache(samp_f32[1], sb_tail), sample_cache(samp_f32[3], sb_tail))
```

```python
import functools
import math

import jax
import jax.numpy as jnp
from jax import lax
from jax.experimental import pallas as pl
from jax.experimental.pallas import tpu as pltpu

D_MODEL = 1024
CHUNK = 64
CHUNK_SHIFT = 6
N_META = 16
HEAD_DIM = 64
N_HEADS_DIFF = 4
N_HEADS_SB = 8
DIFF_WIDTH = N_HEADS_DIFF * 2 * HEAD_DIM
SB_WIDTH = N_HEADS_SB * HEAD_DIM
MIX_WIDTH = DIFF_WIDTH + SB_WIDTH
D_IN = 3 * MIX_WIDTH
D_FF = 4 * D_MODEL
ROPE_THETA = 10000.0
NORM_EPS = 1e-6
NEG_INF = -1e30
LAMBDA_INIT = 0.8 - 0.6 * math.exp(-0.3 * 0)
LOG2_E = math.log2(math.e)

LANES = 128
META_PAD = 128
SB_LOG2_STOP = -58.0
SCORE_LEAD = 3

VMEM_LIMIT = 56 * 1024 * 1024

_f32 = jnp.float32
_bf16 = jnp.bfloat16


def _dot(a, b):
    return jnp.dot(a, b, preferred_element_type=_f32)


def _dot_nt(a, b):
    return lax.dot_general(a, b, (((1,), (1,)), ((), ())), preferred_element_type=_f32)


def _rms(x, g):
    return x * lax.rsqrt(jnp.mean(x * x, axis=-1, keepdims=True) + NORM_EPS) * g


def _log2_sigmoid(s2):
    return jnp.minimum(s2, 0.0) - jnp.log2(1.0 + jnp.exp2(-jnp.abs(s2)))


def _first_half_lanes(shape):
    lane = lax.broadcasted_iota(jnp.int32, shape, 1)
    return (lane & (HEAD_DIM - 1)) < (HEAD_DIM // 2)


def _proj_math(x_ref, g_ref, w_ref, cos, sin):
    h = _rms(x_ref[...], g_ref[...]).astype(_bf16)
    first_half = _first_half_lanes(cos.shape)

    def rope(z):
        parts = []
        for j in range(z.shape[1] // LANES):
            zj = z[:, j * LANES:(j + 1) * LANES]
            swapped = jnp.where(first_half,
                                pltpu.roll(zj, LANES - HEAD_DIM // 2, 1),
                                pltpu.roll(zj, HEAD_DIM // 2, 1))
            parts.append(zj * cos + swapped * sin)
        return jnp.concatenate(parts, axis=1)

    def cols(c):
        return _dot(h, w_ref[:, c * DIFF_WIDTH:(c + 1) * DIFF_WIDTH])

    scale = HEAD_DIM ** -0.5
    scale = scale * LOG2_E
    return (rope(cols(0)) * scale, cols(1) * scale, rope(cols(2)), cols(3), cols(4), cols(5))


def _proj_kernel(x_ref, g_ref, w_ref, cos_ref, sin_ref,
                 qd_ref, qs_ref, kd_ref, ks_ref, vd_ref, vs_ref,
                 kdb_ref, ksb_ref, vdb_ref, vsb_ref):
    qd, qs, kd, ks, vd, vs = _proj_math(x_ref, g_ref, w_ref, cos_ref[...], sin_ref[...])
    qd_ref[...] = qd.astype(_bf16)
    qs_ref[...] = qs.astype(_bf16)
    for f32_ref, b16_ref, val in ((kd_ref, kdb_ref, kd), (ks_ref, ksb_ref, ks),
                                  (vd_ref, vdb_ref, vd), (vs_ref, vsb_ref, vs)):
        f32_ref[...] = val
        b16_ref[...] = val.astype(_bf16)


def _proj(x, g, w_bf16, cos, sin):
    rows = x.shape[0]
    full = lambda a: pl.BlockSpec(a.shape, lambda i: (0, 0))
    blk = lambda width: pl.BlockSpec((rows, width), lambda i: (0, 0))
    f32_out = jax.ShapeDtypeStruct((rows, DIFF_WIDTH), _f32)
    b16_out = jax.ShapeDtypeStruct((rows, DIFF_WIDTH), _bf16)
    return pl.pallas_call(
        _proj_kernel,
        grid=(1,),
        in_specs=[full(x), full(g), full(w_bf16), full(cos), full(sin)],
        out_specs=[blk(DIFF_WIDTH)] * 10,
        out_shape=[b16_out, b16_out, f32_out, f32_out, f32_out, f32_out,
                   b16_out, b16_out, b16_out, b16_out],
        compiler_params=pltpu.CompilerParams(
            dimension_semantics=("arbitrary",), vmem_limit_bytes=VMEM_LIMIT),
        name="proj",
    )(x, g, w_bf16, cos, sin)


def _proj_prompt_kernel(x_ref, g_ref, w_ref, cos_row_ref, sin_row_ref, cos_blk_ref, sin_blk_ref,
                        mkd_ref, mks_ref, mvd_ref, mvs_ref,
                        qd_ref, qs_ref, kd_ref, ks_ref, vd_ref, vs_ref,
                        kdb_ref, ksb_ref, vdb_ref, vsb_ref, tail_ref, *, tm):
    i = pl.program_id(0)
    last = pl.num_programs(0) - 1
    row_outs = ((0, kd_ref), (1, ks_ref), (3, vs_ref))

    def put_vd(token0, rows):
        for h in range(N_HEADS_DIFF):
            vd_ref[pl.ds(token0 * N_HEADS_DIFF + h, rows.shape[0], stride=N_HEADS_DIFF), :] = (
                rows[:, h * LANES:(h + 1) * LANES])

    @pl.when(i == 0)
    def _():
        for j, meta in enumerate((mkd_ref, mks_ref, mvd_ref, mvs_ref)):
            tail_ref[j] = meta[...]

    @pl.when(i < last)
    def _():
        cos_r, sin_r, cos_b, sin_b = cos_row_ref[...], sin_row_ref[...], cos_blk_ref[0], sin_blk_ref[0]
        cos = cos_b * cos_r - sin_b * sin_r
        sin = sin_b * cos_r + cos_b * sin_r
        sin = jnp.where(_first_half_lanes(sin.shape), -sin, sin)
        qd, qs, kd, ks, vd, vs = _proj_math(x_ref, g_ref, w_ref, cos, sin)
        qd_ref[...] = qd.astype(_bf16)
        qs_ref[...] = qs.astype(_bf16)
        for b16_ref, val in ((kdb_ref, kd), (ksb_ref, ks), (vdb_ref, vd), (vsb_ref, vs)):
            b16_ref[...] = val.astype(_bf16)
        for (j, out), val in zip(row_outs, (kd, ks, vs)):
            out[:N_META, :] = tail_ref[j]
            out[N_META:, :] = val[:tm - N_META]
            tail_ref[j] = val[tm - N_META:]
        put_vd(0, tail_ref[2])
        put_vd(N_META, vd[:tm - N_META])
        tail_ref[2] = vd[tm - N_META:]

    @pl.when(i == last)
    def _():
        for j, out in row_outs:
            out[:N_META, :] = tail_ref[j]
        put_vd(0, tail_ref[2])


def _proj_prompt(x, g, w_bf16, pos0, meta_kv, tm):
    rows = x.shape[0]
    nblk = rows // tm
    cos_row, sin_row = _angle_tables(jnp.arange(tm))
    cos_blk, sin_blk = [t[:, None, :] for t in _angle_tables(pos0 + tm * jnp.arange(nblk))]
    frame_blk = lambda width: pl.BlockSpec((tm, width), lambda i: (jnp.minimum(i, nblk - 1), 0))
    blk_angle = pl.BlockSpec((1, 1, LANES), lambda i: (jnp.minimum(i, nblk - 1), 0, 0))
    full = lambda a: pl.BlockSpec(a.shape, lambda i: (0, 0))
    f32_out = jax.ShapeDtypeStruct((N_META + rows, DIFF_WIDTH), _f32)
    vd_out = jax.ShapeDtypeStruct(((N_META + rows) * N_HEADS_DIFF, LANES), _f32)
    b16_out = jax.ShapeDtypeStruct((rows, DIFF_WIDTH), _bf16)
    stream_blk = pl.BlockSpec((tm, DIFF_WIDTH), lambda i: (i, 0))
    return pl.pallas_call(
        functools.partial(_proj_prompt_kernel, tm=tm),
        grid=(nblk + 1,),
        in_specs=[frame_blk(D_MODEL), full(g), full(w_bf16), full(cos_row), full(sin_row),
                  blk_angle, blk_angle] + [full(m) for m in meta_kv],
        out_specs=[frame_blk(DIFF_WIDTH)] * 2
                  + [stream_blk, stream_blk,
                     pl.BlockSpec((tm * N_HEADS_DIFF, LANES), lambda i: (i, 0)), stream_blk]
                  + [frame_blk(DIFF_WIDTH)] * 4,
        out_shape=[b16_out, b16_out, f32_out, f32_out, vd_out, f32_out,
                   b16_out, b16_out, b16_out, b16_out],
        scratch_shapes=[pltpu.VMEM((4, N_META, DIFF_WIDTH), _f32)],
        compiler_params=pltpu.CompilerParams(
            dimension_semantics=("arbitrary",), vmem_limit_bytes=VMEM_LIMIT),
        name="proj_prompt",
    )(x, g, w_bf16, cos_row, sin_row, cos_blk, sin_blk, *meta_kv)


def _angle_tables(pos):
    half = HEAD_DIM // 2
    inv_freq = ROPE_THETA ** (-jnp.arange(half, dtype=_f32) / half)
    ang = pos.astype(_f32)[:, None] * inv_freq[None, :]
    reps = LANES // half
    return jnp.tile(jnp.cos(ang), (1, reps)), jnp.tile(jnp.sin(ang), (1, reps))


def _rope_tables(pos):
    cos, sin = _angle_tables(pos)
    return cos, jnp.where(_first_half_lanes(sin.shape), -sin, sin)


def _lane_half_masks(shape):
    lane = lax.broadcasted_iota(jnp.int32, shape, 1)
    return lane < HEAD_DIM, lane >= HEAD_DIM


def _lambda(lq1, lk1, lq2, lk2):
    return (jnp.exp(jnp.sum(lq1 * lk1, axis=-1, keepdims=True))
            - jnp.exp(jnp.sum(lq2 * lk2, axis=-1, keepdims=True)) + LAMBDA_INIT)


def _diff_finish(o0, l0, o1, l1, lam, g_head):
    o = o0 / l0 - lam * (o1 / l1)
    return _rms(o, g_head) * (1.0 - LAMBDA_INIT)


def _diff_kernel(q_ref, k_ref, v_ref, km_ref, vm_ref, lq1_ref, lk1_ref, lq2_ref, lk2_ref, gh_ref,
                 o_ref, m_ref, acc_ref, *, bq, bk, heads):
    qi = pl.program_id(1)
    lo, hi = _lane_half_masks((bq, LANES))
    zero = jnp.zeros((bq, LANES), _bf16)
    chains = []
    for a in range(heads):
        sl = slice(a * LANES, (a + 1) * LANES)
        q = q_ref[:, sl]
        chains.append((2 * a, sl, jnp.where(lo, q, zero)))
        chains.append((2 * a + 1, sl, jnp.where(hi, q, zero)))

    def softmax_pv(c, s, v, mask, first):
        if mask is not None:
            s = jnp.where(mask, s, NEG_INF)
        m_cur = jnp.max(s, axis=-1, keepdims=True)
        m_new = jnp.broadcast_to(m_cur, (bq, LANES)) if first else jnp.maximum(m_ref[c], m_cur)
        p = jnp.exp2((s - jnp.tile(m_new, (1, s.shape[1] // LANES))).astype(_bf16))
        pv = _dot(p, jnp.concatenate([v, jnp.ones_like(v)], axis=1))
        if first:
            acc_ref[c] = pv
        else:
            alpha = jnp.exp2(m_ref[c] - m_new)
            acc_ref[c] = jnp.tile(alpha, (1, 2)) * acc_ref[c] + pv
        m_ref[c] = m_new

    def step(k_of, v_of, mask, first):
        scores = {}
        for i in range(len(chains) + SCORE_LEAD):
            if i < len(chains):
                _, sl, qz = chains[i]
                scores[i] = _dot_nt(qz, k_of(sl))
            if i >= SCORE_LEAD:
                c, sl, _ = chains[i - SCORE_LEAD]
                softmax_pv(c, scores.pop(i - SCORE_LEAD), v_of(sl), mask, first)

    q0 = qi * bq
    n_full = (q0 + CHUNK) // bk

    start = pl.multiple_of(n_full * bk, bk)
    row = lax.broadcasted_iota(jnp.int32, (bq, bk + META_PAD), 0) + q0
    col = lax.broadcasted_iota(jnp.int32, (bq, bk + META_PAD), 1)
    key_chunk = jnp.where(col < bk, (col + start) >> CHUNK_SHIFT,
                          jnp.where(col < bk + N_META, -1, jnp.iinfo(jnp.int32).max))
    mask = key_chunk <= (row >> CHUNK_SHIFT)
    step(lambda sl: jnp.concatenate([k_ref[pl.ds(start, bk), sl], km_ref[:, sl]], axis=0),
         lambda sl: jnp.concatenate([v_ref[pl.ds(start, bk), sl], vm_ref[:, sl]], axis=0), mask, True)

    def full_keys(start, width):
        start = pl.multiple_of(start, bk)
        step(lambda sl: k_ref[pl.ds(start, width), sl], lambda sl: v_ref[pl.ds(start, width), sl],
             None, False)

    def tile_pair(j, carry):
        full_keys(j * 2 * bk, 2 * bk)
        return carry

    lax.fori_loop(0, n_full // 2, tile_pair, 0)

    @pl.when(n_full % 2 == 1)
    def _():
        full_keys((n_full - 1) * bk, bk)

    lam = _lambda(lq1_ref[...], lk1_ref[...], lq2_ref[...], lk2_ref[...])
    for a in range(heads):
        acc0, acc1 = acc_ref[2 * a], acc_ref[2 * a + 1]
        out = _diff_finish(acc0[:, :LANES], acc0[:, LANES:], acc1[:, :LANES], acc1[:, LANES:],
                           lam, gh_ref[...])
        o_ref[:, a * LANES:(a + 1) * LANES] = out.astype(_bf16)


def _diff_attn(qd, kdb, vdb, kmeta, vmeta, lq1, lk1, lq2, lk2, g_head, bq=512, bk=512,
               heads=N_HEADS_DIFF):
    n = qd.shape[0]
    width = heads * LANES
    head_blk = lambda rows: pl.BlockSpec((rows, width), lambda h, i: (0, h),
                                         pipeline_mode=pl.Buffered(1))
    small = lambda a: pl.BlockSpec(a.shape, lambda h, i: (0, 0))
    return pl.pallas_call(
        functools.partial(_diff_kernel, bq=bq, bk=bk, heads=heads),
        grid=(N_HEADS_DIFF // heads, n // bq),
        in_specs=[pl.BlockSpec((bq, width), lambda h, i: (i, h)),
                  head_blk(n), head_blk(n), head_blk(META_PAD), head_blk(META_PAD),
                  small(lq1), small(lk1), small(lq2), small(lk2), small(g_head)],
        out_specs=pl.BlockSpec((bq, width), lambda h, i: (i, h)),
        out_shape=jax.ShapeDtypeStruct((n, DIFF_WIDTH), _bf16),
        scratch_shapes=[pltpu.VMEM((2 * heads, bq, LANES), _f32),
                        pltpu.VMEM((2 * heads, bq, 2 * LANES), _f32)],
        compiler_params=pltpu.CompilerParams(
            dimension_semantics=("arbitrary", "arbitrary"), vmem_limit_bytes=VMEM_LIMIT),
        name="diff_attn",
    )(qd, kdb, vdb, kmeta, vmeta, lq1, lk1, lq2, lk2, g_head)


def _suffix_matrix(n):
    j = lax.broadcasted_iota(jnp.int32, (n, n), 0)
    s = lax.broadcasted_iota(jnp.int32, (n, n), 1)
    return jnp.where(j > s, 1.0, 0.0).astype(_bf16)


def _sb_tiles(queries, keys, values, carries, mask, suffix):
    scores = [_dot_nt(q, k) for q, k in zip(queries, keys)]
    log_betas, log_keeps = [], []
    for s in scores:
        log_beta = _log2_sigmoid(s)
        log_keep = log_beta - s
        if mask is not None:
            log_keep = jnp.where(mask, log_keep, 0.0)
        log_betas.append(log_beta)
        log_keeps.append(log_keep)
    afters = [_dot(log_keep.astype(_bf16), suffix) for log_keep in log_keeps]
    weights, new_carries = [], []
    for log_beta, log_keep, after, carry in zip(log_betas, log_keeps, afters, carries):
        a = jnp.exp2(log_beta + after + carry)
        if mask is not None:
            a = jnp.where(mask, a, 0.0)
        weights.append(a.astype(_bf16))
        new_carries.append(carry + after[:, 0:1] + log_keep[:, 0:1])
    return [_dot(a, v) for a, v in zip(weights, values)], new_carries


def _sb_kernel(q_ref, k_ref, v_ref, km_ref, vm_ref, o_ref, acc_ref, carry_ref, live_ref, *, bq):
    qi = pl.program_id(0)
    lo, hi = _lane_half_masks((bq, LANES))
    zero = jnp.zeros((bq, LANES), _bf16)
    lanes, queries = [], []
    for p in range(SB_WIDTH // LANES):
        sl = slice(p * LANES, (p + 1) * LANES)
        q = q_ref[:, sl]
        lanes += [sl, sl]
        queries += [jnp.where(lo, q, zero), jnp.where(hi, q, zero)]
    heads = range(N_HEADS_SB)
    bk = 2 * bq

    def tile(k_of, v_of, mask, suffix, first):
        carries = [jnp.zeros((bq, 1), _f32) if first else carry_ref[c] for c in heads]
        pvs, carries = _sb_tiles(queries, [k_of(sl) for sl in lanes], [v_of(sl) for sl in lanes],
                                 carries, mask, suffix)
        live = carries[0]
        for c in heads:
            acc_ref[c] = pvs[c] if first else acc_ref[c] + pvs[c]
            carry_ref[c] = carries[c]
            live = jnp.maximum(live, carries[c])
        return jnp.max(live)

    def frame_tile(start, width, mask, first):
        start = pl.multiple_of(start, bq)
        return tile(lambda sl: k_ref[pl.ds(start, width), sl], lambda sl: v_ref[pl.ds(start, width), sl],
                    mask, _suffix_matrix(width), first)

    q0 = qi * bq
    start0 = jnp.maximum(q0 - bq, 0)
    row = lax.broadcasted_iota(jnp.int32, (bq, bk), 0) + q0
    col = lax.broadcasted_iota(jnp.int32, (bq, bk), 1) + start0
    live = frame_tile(start0, bk, col < row, True)

    def cond(state):
        start, live = state
        return jnp.logical_and(start >= bk, live > SB_LOG2_STOP)

    def body(state):
        start, _ = state
        return start - bk, frame_tile(start - bk, bk, None, False)

    start, live = lax.while_loop(cond, body, (start0, live))
    live_ref[0] = live

    @pl.when(jnp.logical_and(start == bq, live > SB_LOG2_STOP))
    def _():
        live_ref[0] = frame_tile(0, bq, None, False)

    @pl.when(jnp.logical_and(start < bk, live_ref[0] > SB_LOG2_STOP))
    def _():
        mcol = lax.broadcasted_iota(jnp.int32, (bq, META_PAD), 1)
        tile(lambda sl: km_ref[:, sl], lambda sl: vm_ref[:, sl], mcol < N_META,
             _suffix_matrix(META_PAD), False)

    for p in range(SB_WIDTH // LANES):
        o_ref[:, p * LANES:(p + 1) * LANES] = jnp.where(lo, acc_ref[2 * p], acc_ref[2 * p + 1]).astype(_bf16)


def _sb_attn(qs, ksb, vsb, kmeta, vmeta, bq=128):
    n = qs.shape[0]
    resident = lambda a: pl.BlockSpec(a.shape, lambda i: (0, 0), pipeline_mode=pl.Buffered(1))
    return pl.pallas_call(
        functools.partial(_sb_kernel, bq=bq),
        grid=(n // bq,),
        in_specs=[pl.BlockSpec((bq, SB_WIDTH), lambda i: (i, 0)),
                  resident(ksb), resident(vsb), resident(kmeta), resident(vmeta)],
        out_specs=pl.BlockSpec((bq, SB_WIDTH), lambda i: (i, 0)),
        out_shape=jax.ShapeDtypeStruct((n, SB_WIDTH), _bf16),
        scratch_shapes=[pltpu.VMEM((N_HEADS_SB, bq, LANES), _f32),
                        pltpu.VMEM((N_HEADS_SB, bq, 1), _f32),
                        pltpu.SMEM((1,), _f32)],
        compiler_params=pltpu.CompilerParams(
            dimension_semantics=("arbitrary",), vmem_limit_bytes=VMEM_LIMIT),
        name="sb_attn",
    )(qs, ksb, vsb, kmeta, vmeta)


def _samp_kernel(qd_ref, qs_ref, kdn_ref, ksn_ref, vdn_ref, vsn_ref,
                 kdm_ref, ksm_ref, vdm_ref, vsm_ref,
                 ckd_ref, cks_ref, cvd_ref, cvs_ref,
                 lq1_ref, lk1_ref, lq2_ref, lk2_ref, gh_ref,
                 od_ref, os_ref, kd_edge, ks_edge, vd_edge, vs_edge, *, ts, past):
    for edge, meta, new in ((kd_edge, kdm_ref, kdn_ref), (ks_edge, ksm_ref, ksn_ref),
                            (vd_edge, vdm_ref, vdn_ref), (vs_edge, vsm_ref, vsn_ref)):
        edge[0:N_META, :] = meta[0:N_META, :]
        edge[N_META:N_META + ts, :] = new[...]
        edge[N_META + ts:, :] = jnp.zeros((LANES - N_META - ts, DIFF_WIDTH), _bf16)

    grp = 2 * ts
    n_cols = past + LANES
    col = lax.broadcasted_iota(jnp.int32, (grp, n_cols), 1)
    ecol = col - past
    pos_k = jnp.where(ecol < 0, col + N_META, jnp.where(ecol < N_META, ecol, ecol + past))
    pos_q = (lax.broadcasted_iota(jnp.int32, (grp, n_cols), 0) & (ts - 1)) + N_META + past
    valid = ecol < N_META + ts
    chunk = lambda p: jnp.where(p < N_META, -1, (p - N_META) >> CHUNK_SHIFT)
    mask_chunk = jnp.logical_and(valid, chunk(pos_k) <= chunk(pos_q))
    mask_strict = jnp.logical_and(valid, pos_k < pos_q)

    lo, hi = _lane_half_masks((ts, LANES))
    zero = jnp.zeros((ts, LANES), _bf16)
    blocks = [slice(j * LANES, (j + 1) * LANES) for j in range(DIFF_WIDTH // LANES)]

    def stacked(q):
        return jnp.concatenate([jnp.where(lo, q, zero), jnp.where(hi, q, zero)], axis=0)

    def scores(q_ref, cache_ref, edge):
        out = []
        for sl in blocks:
            q = stacked(q_ref[:, sl])
            out.append(jnp.concatenate([_dot(q, cache_ref[0, sl, :].astype(_bf16)),
                                        _dot_nt(q, edge[:, sl])], axis=1))
        return out

    diff_scores = scores(qd_ref, ckd_ref, kd_edge)
    sb_scores = scores(qs_ref, cks_ref, ks_edge)

    lam = _lambda(lq1_ref[...], lk1_ref[...], lq2_ref[...], lk2_ref[...])
    probs, sums = [], []
    for s in diff_scores:
        s = jnp.where(mask_chunk, s, NEG_INF)
        p = jnp.exp2(s - jnp.max(s, axis=-1, keepdims=True))
        probs.append(p.astype(_bf16))
        sums.append(jnp.sum(p, axis=-1, keepdims=True))
    for h, (sl, p, l) in enumerate(zip(blocks, probs, sums)):
        v_cache = cvd_ref[0, pl.ds(h, past, stride=N_HEADS_DIFF), :].astype(_bf16)
        pv = _dot(p[:, :past], v_cache) + _dot(p[:, past:], vd_edge[:, sl])
        out = _diff_finish(pv[:ts], l[:ts], pv[ts:], l[ts:], lam, gh_ref[...])
        od_ref[:, sl] = out.astype(_bf16)

    n_tiles = past // LANES
    tiles = [slice(t * LANES, (t + 1) * LANES) for t in range(n_tiles)]
    edge_cols = slice(past, n_cols)
    log_betas, log_keeps, cache_pieces, edge_pieces = [], [], [], []
    for s in sb_scores:
        log_beta = _log2_sigmoid(s)
        log_keep = jnp.where(mask_strict, log_beta - s, 0.0)
        keep_hi = log_keep.astype(_bf16)
        keep_lo = (log_keep - keep_hi.astype(_f32)).astype(_bf16)
        log_betas.append(log_beta)
        log_keeps.append(log_keep)
        cache_pieces += [keep_hi[:, t] for t in tiles] + [keep_lo[:, t] for t in tiles]
        edge_pieces += [keep_hi[:, edge_cols], keep_lo[:, edge_cols]]
    after_cache = _dot(jnp.concatenate(cache_pieces, axis=0), _suffix_matrix(LANES))
    ej = lax.broadcasted_iota(jnp.int32, (LANES, LANES), 0)
    es = lax.broadcasted_iota(jnp.int32, (LANES, LANES), 1)
    edge_suffix = jnp.where(jnp.logical_and(ej > es, (ej < N_META) == (es < N_META)), 1.0, 0.0)
    after_edge = _dot(jnp.concatenate(edge_pieces, axis=0), edge_suffix.astype(_bf16))
    is_meta = lax.broadcasted_iota(jnp.int32, (grp, LANES), 1) < N_META
    for j, sl in enumerate(blocks):
        base = j * 2 * n_tiles
        after = [after_cache[(base + t) * grp:(base + t + 1) * grp]
                 + after_cache[(base + n_tiles + t) * grp:(base + n_tiles + t + 1) * grp]
                 for t in range(n_tiles)]
        edge_after = (after_edge[2 * j * grp:(2 * j + 1) * grp]
                      + after_edge[(2 * j + 1) * grp:(2 * j + 2) * grp])
        keep_edge = log_keeps[j][:, edge_cols]
        carry = edge_after[:, N_META:N_META + 1] + keep_edge[:, N_META:N_META + 1]
        shifted = [None] * n_tiles
        for t in reversed(range(n_tiles)):
            shifted[t] = after[t] + carry
            carry = carry + after[t][:, 0:1] + log_keeps[j][:, tiles[t]][:, 0:1]
        shifted.append(edge_after + jnp.where(is_meta, carry, 0.0))
        a = jnp.exp2(log_betas[j] + jnp.concatenate(shifted, axis=1))
        a = jnp.where(mask_strict, a, 0.0).astype(_bf16)
        pv = _dot_nt(a[:, :past], cvs_ref[0, sl, :].astype(_bf16)) + _dot(a[:, past:], vs_edge[:, sl])
        os_ref[:, sl] = jnp.where(lo, pv[:ts], pv[ts:]).astype(_bf16)


def _samp_attn(qd, qs, kdn, ksn, vdn, vsn, kdm, ksm, vdm, vsm, ckd_t, cks_t, cvd_rows, cvs_t,
               lq1, lk1, lq2, lk2, g_head, ts):
    nb, _, past = ckd_t.shape
    assert past % LANES == 0 and N_META + ts <= LANES
    row_blk = pl.BlockSpec((ts, DIFF_WIDTH), lambda b: (b, 0))
    per_batch = lambda a: pl.BlockSpec((1,) + a.shape[1:], lambda b: (b, 0, 0))
    small = lambda a: pl.BlockSpec(a.shape, lambda b: (0, 0))
    out = jax.ShapeDtypeStruct((nb * ts, DIFF_WIDTH), _bf16)
    return pl.pallas_call(
        functools.partial(_samp_kernel, ts=ts, past=past),
        grid=(nb,),
        in_specs=[row_blk] * 6 + [small(kdm)] * 4
                 + [per_batch(ckd_t), per_batch(cks_t), per_batch(cvd_rows), per_batch(cvs_t)]
                 + [small(lq1), small(lk1), small(lq2), small(lk2), small(g_head)],
        out_specs=[row_blk, row_blk],
        out_shape=[out, out],
        scratch_shapes=[pltpu.VMEM((LANES, DIFF_WIDTH), _bf16)] * 4,
        compiler_params=pltpu.CompilerParams(
            dimension_semantics=("arbitrary",), vmem_limit_bytes=VMEM_LIMIT),
        name="samp_attn",
    )(qd, qs, kdn, ksn, vdn, vsn, kdm, ksm, vdm, vsm, ckd_t, cks_t, cvd_rows, cvs_t,
      lq1, lk1, lq2, lk2, g_head)


def _mlp_kernel(x_ref, md_ref, ms_ref, wo_ref, gm_ref, wu_ref, wd_ref, gf_ref, o_ref, h2_ref, *, tf):
    x1 = (x_ref[...] + _dot(md_ref[...], wo_ref[:DIFF_WIDTH, :])
          + _dot(ms_ref[...], wo_ref[DIFF_WIDTH:, :]))
    h2_ref[...] = _rms(x1, gm_ref[...]).astype(_bf16)
    o_ref[...] = x1
    for f in range(D_FF // tf):
        u = jnp.maximum(_dot(h2_ref[...], wu_ref[:, f * tf:(f + 1) * tf]), 0.0)
        o_ref[...] += _dot((u * u).astype(_bf16), wd_ref[f * tf:(f + 1) * tf, :])
    o_ref[...] = _rms(o_ref[...], gf_ref[...])


def _mlp(x, mixed_d, mixed_s, wo, g_mlp, wu, wd, g_final, tm=512, tf=1024):
    rows = x.shape[0]
    resident = lambda a: pl.BlockSpec(a.shape, lambda i: (0, 0), pipeline_mode=pl.Buffered(1))
    small = lambda a: pl.BlockSpec(a.shape, lambda i: (0, 0))
    row_blk = lambda width: pl.BlockSpec((tm, width), lambda i: (i, 0))
    return pl.pallas_call(
        functools.partial(_mlp_kernel, tf=tf),
        grid=(rows // tm,),
        in_specs=[row_blk(D_MODEL), row_blk(DIFF_WIDTH), row_blk(SB_WIDTH),
                  resident(wo), small(g_mlp), resident(wu), resident(wd), small(g_final)],
        out_specs=row_blk(D_MODEL),
        out_shape=jax.ShapeDtypeStruct((rows, D_MODEL), _f32),
        scratch_shapes=[pltpu.VMEM((tm, D_MODEL), _bf16)],
        compiler_params=pltpu.CompilerParams(
            dimension_semantics=("arbitrary",), vmem_limit_bytes=VMEM_LIMIT),
        name="mlp",
    )(x, mixed_d, mixed_s, wo, g_mlp, wu, wd, g_final)


def kernel(x_prompt, x_sample, cache_diff_k, cache_diff_v, cache_sb_k, cache_sb_v, meta_tokens, g_mix, w_in, lambda_q1, lambda_k1, lambda_q2, lambda_k2, g_diff_head, w_out, g_mlp, w_up, w_down, g_final):
    bp, tp, d = x_prompt.shape
    bs, ts, _ = x_sample.shape
    past = cache_diff_k.shape[2]
    assert bp == 1 and g_mix.shape[0] == 1, "single prompt sequence, depth 1"

    w_in_b = w_in[0].astype(_bf16)
    w_out_b = w_out[0].astype(_bf16)
    w_up_b = w_up[0].astype(_bf16)
    w_down_b = w_down[0].astype(_bf16)
    g_final2 = g_final[None, :]
    lams = (lambda_q1, lambda_k1, lambda_q2, lambda_k2)

    x_ms = jnp.concatenate([meta_tokens.astype(x_sample.dtype), x_sample.reshape(bs * ts, d)], axis=0)
    pos_ms = jnp.concatenate([jnp.arange(N_META), jnp.tile(N_META + past + jnp.arange(ts), bs)])
    cos_s, sin_s = _rope_tables(pos_ms)
    s_out = _proj(x_ms, g_mix, w_in_b, cos_s, sin_s)
    meta_f32 = [a[:N_META] for a in s_out[2:6]]
    samp_f32 = [a[N_META:] for a in s_out[2:6]]
    pad_meta = lambda a: jnp.pad(a[:N_META], ((0, META_PAD - N_META), (0, 0)))
    kdm, ksm, vdm, vsm = [pad_meta(a) for a in s_out[6:10]]
    qd_s, qs_s = s_out[0][N_META:], s_out[1][N_META:]
    kdn, ksn, vdn, vsn = [a[N_META:] for a in s_out[6:10]]

    p_out = _proj_prompt(x_prompt[0], g_mix, w_in_b, N_META, meta_f32, tm=512)
    qd, qs, kd, ks, vd, vs, kdb, ksb, vdb, vsb = p_out

    mixed_d = _diff_attn(qd, kdb, vdb, kdm, vdm, *lams, g_diff_head)
    mixed_s = _sb_attn(qs, ksb, vsb, ksm, vsm)
    y_prompt = _mlp(x_prompt[0], mixed_d, mixed_s, w_out_b, g_mlp, w_up_b, w_down_b, g_final2)

    feat_major = lambda c: jnp.transpose(c[0].reshape(bs, past, DIFF_WIDTH), (0, 2, 1))
    smix_d, smix_s = _samp_attn(qd_s, qs_s, kdn, ksn, vdn, vsn, kdm, ksm, vdm, vsm,
                                feat_major(cache_diff_k), feat_major(cache_sb_k),
                                cache_diff_v[0].reshape(bs, past * N_HEADS_DIFF, 2 * HEAD_DIM),
                                feat_major(cache_sb_v), *lams, g_diff_head, ts)
    y_sample = _mlp(x_sample.reshape(bs * ts, d), smix_d, smix_s, w_out_b, g_mlp, w_up_b, w_down_b,
                    g_final2)

    def prompt_cache(a, tail):
        return a.reshape((1, bp, N_META + tp) + tail)

    def sample_cache(a, tail):
        return a.reshape((1, bs, ts) + tail)

    dk_tail, dv_tail = (N_HEADS_DIFF, 2, HEAD_DIM), (N_HEADS_DIFF, 2 * HEAD_DIM)
    sb_tail = (N_HEADS_SB, HEAD_DIM)
    return (y_prompt[None], y_sample.reshape(bs, ts, d),
            prompt_cache(kd, dk_tail), prompt_cache(vd, dv_tail),
            prompt_cache(ks, sb_tail), prompt_cache(vs, sb_tail),
            sample_cache(samp_f32[0], dk_tail), sample_cache(samp_f32[2], dv_tail),
            sample_cache(samp_f32[1], sb_tail), sample_cache(samp_f32[3], sb_tail))
```

```python
import functools
import math

import jax
import jax.numpy as jnp
from jax import lax
from jax.experimental import pallas as pl
from jax.experimental.pallas import tpu as pltpu

D_MODEL = 1024
CHUNK = 64
CHUNK_SHIFT = 6
N_META = 16
HEAD_DIM = 64
N_HEADS_DIFF = 4
N_HEADS_SB = 8
DIFF_WIDTH = N_HEADS_DIFF * 2 * HEAD_DIM
SB_WIDTH = N_HEADS_SB * HEAD_DIM
MIX_WIDTH = DIFF_WIDTH + SB_WIDTH
D_IN = 3 * MIX_WIDTH
D_FF = 4 * D_MODEL
ROPE_THETA = 10000.0
NORM_EPS = 1e-6
NEG_INF = -1e30
LAMBDA_INIT = 0.8 - 0.6 * math.exp(-0.3 * 0)
LOG2_E = math.log2(math.e)

LANES = 128
META_PAD = 128
SB_LOG2_STOP = -58.0
SCORE_LEAD = 3

VMEM_LIMIT = 56 * 1024 * 1024

_f32 = jnp.float32
_bf16 = jnp.bfloat16


def _dot(a, b):
    return jnp.dot(a, b, preferred_element_type=_f32)


def _dot_nt(a, b):
    return lax.dot_general(a, b, (((1,), (1,)), ((), ())), preferred_element_type=_f32)


def _rms(x, g):
    return x * lax.rsqrt(jnp.mean(x * x, axis=-1, keepdims=True) + NORM_EPS) * g


def _log2_sigmoid(s2):
    return jnp.minimum(s2, 0.0) - jnp.log2(1.0 + jnp.exp2(-jnp.abs(s2)))


def _first_half_lanes(shape):
    lane = lax.broadcasted_iota(jnp.int32, shape, 1)
    return (lane & (HEAD_DIM - 1)) < (HEAD_DIM // 2)


def _proj_math(x_ref, g_ref, w_ref, cos, sin):
    h = _rms(x_ref[...], g_ref[...]).astype(_bf16)
    first_half = _first_half_lanes(cos.shape)

    def rope(z):
        parts = []
        for j in range(z.shape[1] // LANES):
            zj = z[:, j * LANES:(j + 1) * LANES]
            swapped = jnp.where(first_half,
                                pltpu.roll(zj, LANES - HEAD_DIM // 2, 1),
                                pltpu.roll(zj, HEAD_DIM // 2, 1))
            parts.append(zj * cos + swapped * sin)
        return jnp.concatenate(parts, axis=1)

    def cols(c):
        return _dot(h, w_ref[:, c * DIFF_WIDTH:(c + 1) * DIFF_WIDTH])

    scale = HEAD_DIM ** -0.5
    scale = scale * LOG2_E
    return (rope(cols(0)) * scale, cols(1) * scale, rope(cols(2)), cols(3), cols(4), cols(5))


def _proj_kernel(x_ref, g_ref, w_ref, cos_ref, sin_ref,
                 qd_ref, qs_ref, kd_ref, ks_ref, vd_ref, vs_ref,
                 kdb_ref, ksb_ref, vdb_ref, vsb_ref):
    qd, qs, kd, ks, vd, vs = _proj_math(x_ref, g_ref, w_ref, cos_ref[...], sin_ref[...])
    qd_ref[...] = qd.astype(_bf16)
    qs_ref[...] = qs.astype(_bf16)
    for f32_ref, b16_ref, val in ((kd_ref, kdb_ref, kd), (ks_ref, ksb_ref, ks),
                                  (vd_ref, vdb_ref, vd), (vs_ref, vsb_ref, vs)):
        f32_ref[...] = val
        b16_ref[...] = val.astype(_bf16)


def _proj(x, g, w_bf16, cos, sin):
    rows = x.shape[0]
    full = lambda a: pl.BlockSpec(a.shape, lambda i: (0, 0))
    blk = lambda width: pl.BlockSpec((rows, width), lambda i: (0, 0))
    f32_out = jax.ShapeDtypeStruct((rows, DIFF_WIDTH), _f32)
    b16_out = jax.ShapeDtypeStruct((rows, DIFF_WIDTH), _bf16)
    return pl.pallas_call(
        _proj_kernel,
        grid=(1,),
        in_specs=[full(x), full(g), full(w_bf16), full(cos), full(sin)],
        out_specs=[blk(DIFF_WIDTH)] * 10,
        out_shape=[b16_out, b16_out, f32_out, f32_out, f32_out, f32_out,
                   b16_out, b16_out, b16_out, b16_out],
        compiler_params=pltpu.CompilerParams(
            dimension_semantics=("arbitrary",), vmem_limit_bytes=VMEM_LIMIT),
        name="proj",
    )(x, g, w_bf16, cos, sin)


def _proj_prompt_kernel(x_ref, g_ref, w_ref, cos_row_ref, sin_row_ref, cos_blk_ref, sin_blk_ref,
                        mkd_ref, mks_ref, mvd_ref, mvs_ref,
                        qd_ref, qs_ref, kd_ref, ks_ref, vd_ref, vs_ref,
                        kdb_ref, ksb_ref, vdb_ref, vsb_ref, tail_ref, *, tm):
    i = pl.program_id(0)
    last = pl.num_programs(0) - 1
    row_outs = ((0, kd_ref), (1, ks_ref), (3, vs_ref))

    def put_vd(token0, rows):
        for h in range(N_HEADS_DIFF):
            vd_ref[pl.ds(token0 * N_HEADS_DIFF + h, rows.shape[0], stride=N_HEADS_DIFF), :] = (
                rows[:, h * LANES:(h + 1) * LANES])

    @pl.when(i == 0)
    def _():
        for j, meta in enumerate((mkd_ref, mks_ref, mvd_ref, mvs_ref)):
            tail_ref[j] = meta[...]

    @pl.when(i < last)
    def _():
        cos_r, sin_r, cos_b, sin_b = cos_row_ref[...], sin_row_ref[...], cos_blk_ref[0], sin_blk_ref[0]
        cos = cos_b * cos_r - sin_b * sin_r
        sin = sin_b * cos_r + cos_b * sin_r
        sin = jnp.where(_first_half_lanes(sin.shape), -sin, sin)
        qd, qs, kd, ks, vd, vs = _proj_math(x_ref, g_ref, w_ref, cos, sin)
        qd_ref[...] = qd.astype(_bf16)
        qs_ref[...] = qs.astype(_bf16)
        for b16_ref, val in ((kdb_ref, kd), (ksb_ref, ks), (vdb_ref, vd), (vsb_ref, vs)):
            b16_ref[...] = val.astype(_bf16)
        for (j, out), val in zip(row_outs, (kd, ks, vs)):
            out[:N_META, :] = tail_ref[j]
            out[N_META:, :] = val[:tm - N_META]
            tail_ref[j] = val[tm - N_META:]
        put_vd(0, tail_ref[2])
        put_vd(N_META, vd[:tm - N_META])
        tail_ref[2] = vd[tm - N_META:]

    @pl.when(i == last)
    def _():
        for j, out in row_outs:
            out[:N_META, :] = tail_ref[j]
        put_vd(0, tail_ref[2])


def _proj_prompt(x, g, w_bf16, pos0, meta_kv, tm):
    rows = x.shape[0]
    nblk = rows // tm
    cos_row, sin_row = _angle_tables(jnp.arange(tm))
    cos_blk, sin_blk = [t[:, None, :] for t in _angle_tables(pos0 + tm * jnp.arange(nblk))]
    frame_blk = lambda width: pl.BlockSpec((tm, width), lambda i: (jnp.minimum(i, nblk - 1), 0))
    blk_angle = pl.BlockSpec((1, 1, LANES), lambda i: (jnp.minimum(i, nblk - 1), 0, 0))
    full = lambda a: pl.BlockSpec(a.shape, lambda i: (0, 0))
    f32_out = jax.ShapeDtypeStruct((N_META + rows, DIFF_WIDTH), _f32)
    vd_out = jax.ShapeDtypeStruct(((N_META + rows) * N_HEADS_DIFF, LANES), _f32)
    b16_out = jax.ShapeDtypeStruct((rows, DIFF_WIDTH), _bf16)
    stream_blk = pl.BlockSpec((tm, DIFF_WIDTH), lambda i: (i, 0))
    return pl.pallas_call(
        functools.partial(_proj_prompt_kernel, tm=tm),
        grid=(nblk + 1,),
        in_specs=[frame_blk(D_MODEL), full(g), full(w_bf16), full(cos_row), full(sin_row),
                  blk_angle, blk_angle] + [full(m) for m in meta_kv],
        out_specs=[frame_blk(DIFF_WIDTH)] * 2
                  + [stream_blk, stream_blk,
                     pl.BlockSpec((tm * N_HEADS_DIFF, LANES), lambda i: (i, 0)), stream_blk]
                  + [frame_blk(DIFF_WIDTH)] * 4,
        out_shape=[b16_out, b16_out, f32_out, f32_out, vd_out, f32_out,
                   b16_out, b16_out, b16_out, b16_out],
        scratch_shapes=[pltpu.VMEM((4, N_META, DIFF_WIDTH), _f32)],
        compiler_params=pltpu.CompilerParams(
            dimension_semantics=("arbitrary",), vmem_limit_bytes=VMEM_LIMIT),
        name="proj_prompt",
    )(x, g, w_bf16, cos_row, sin_row, cos_blk, sin_blk, *meta_kv)


def _angle_tables(pos):
    half = HEAD_DIM // 2
    inv_freq = ROPE_THETA ** (-jnp.arange(half, dtype=_f32) / half)
    ang = pos.astype(_f32)[:, None] * inv_freq[None, :]
    reps = LANES // half
    return jnp.tile(jnp.cos(ang), (1, reps)), jnp.tile(jnp.sin(ang), (1, reps))


def _rope_tables(pos):
    cos, sin = _angle_tables(pos)
    return cos, jnp.where(_first_half_lanes(sin.shape), -sin, sin)


def _lane_half_masks(shape):
    lane = lax.broadcasted_iota(jnp.int32, shape, 1)
    return lane < HEAD_DIM, lane >= HEAD_DIM


def _lambda(lq1, lk1, lq2, lk2):
    return (jnp.exp(jnp.sum(lq1 * lk1, axis=-1, keepdims=True))
            - jnp.exp(jnp.sum(lq2 * lk2, axis=-1, keepdims=True)) + LAMBDA_INIT)


def _diff_finish(o0, l0, o1, l1, lam, g_head):
    o = o0 / l0 - lam * (o1 / l1)
    return _rms(o, g_head) * (1.0 - LAMBDA_INIT)


def _diff_kernel(q_ref, k_ref, v_ref, km_ref, vm_ref, lq1_ref, lk1_ref, lq2_ref, lk2_ref, gh_ref,
                 o_ref, m_ref, acc_ref, *, bq, bk, heads):
    qi = pl.program_id(1)
    lo, hi = _lane_half_masks((bq, LANES))
    zero = jnp.zeros((bq, LANES), _bf16)
    chains = []
    for a in range(heads):
        sl = slice(a * LANES, (a + 1) * LANES)
        q = q_ref[:, sl]
        chains.append((2 * a, sl, jnp.where(lo, q, zero)))
        chains.append((2 * a + 1, sl, jnp.where(hi, q, zero)))

    def softmax_pv(c, s, v, mask, first):
        if mask is not None:
            s = jnp.where(mask, s, NEG_INF)
        m_cur = jnp.max(s, axis=-1, keepdims=True)
        m_new = jnp.broadcast_to(m_cur, (bq, LANES)) if first else jnp.maximum(m_ref[c], m_cur)
        p = jnp.exp2((s - jnp.tile(m_new, (1, s.shape[1] // LANES))).astype(_bf16))
        pv = _dot(p, jnp.concatenate([v, jnp.ones_like(v)], axis=1))
        if first:
            acc_ref[c] = pv
        else:
            alpha = jnp.exp2(m_ref[c] - m_new)
            acc_ref[c] = jnp.tile(alpha, (1, 2)) * acc_ref[c] + pv
        m_ref[c] = m_new

    def step(k_of, v_of, mask, first):
        scores = {}
        for i in range(len(chains) + SCORE_LEAD):
            if i < len(chains):
                _, sl, qz = chains[i]
                scores[i] = _dot_nt(qz, k_of(sl))
            if i >= SCORE_LEAD:
                c, sl, _ = chains[i - SCORE_LEAD]
                softmax_pv(c, scores.pop(i - SCORE_LEAD), v_of(sl), mask, first)

    q0 = qi * bq
    n_full = (q0 + CHUNK) // bk

    start = pl.multiple_of(n_full * bk, bk)
    row = lax.broadcasted_iota(jnp.int32, (bq, bk + META_PAD), 0) + q0
    col = lax.broadcasted_iota(jnp.int32, (bq, bk + META_PAD), 1)
    key_chunk = jnp.where(col < bk, (col + start) >> CHUNK_SHIFT,
                          jnp.where(col < bk + N_META, -1, jnp.iinfo(jnp.int32).max))
    mask = key_chunk <= (row >> CHUNK_SHIFT)
    step(lambda sl: jnp.concatenate([k_ref[pl.ds(start, bk), sl], km_ref[:, sl]], axis=0),
         lambda sl: jnp.concatenate([v_ref[pl.ds(start, bk), sl], vm_ref[:, sl]], axis=0), mask, True)

    def full_keys(start, width):
        start = pl.multiple_of(start, bk)
        step(lambda sl: k_ref[pl.ds(start, width), sl], lambda sl: v_ref[pl.ds(start, width), sl],
             None, False)

    def tile_pair(j, carry):
        full_keys(j * 2 * bk, 2 * bk)
        return carry

    lax.fori_loop(0, n_full // 2, tile_pair, 0)

    @pl.when(n_full % 2 == 1)
    def _():
        full_keys((n_full - 1) * bk, bk)

    lam = _lambda(lq1_ref[...], lk1_ref[...], lq2_ref[...], lk2_ref[...])
    for a in range(heads):
        acc0, acc1 = acc_ref[2 * a], acc_ref[2 * a + 1]
        out = _diff_finish(acc0[:, :LANES], acc0[:, LANES:], acc1[:, :LANES], acc1[:, LANES:],
                           lam, gh_ref[...])
        o_ref[:, a * LANES:(a + 1) * LANES] = out.astype(_bf16)


def _diff_attn(qd, kdb, vdb, kmeta, vmeta, lq1, lk1, lq2, lk2, g_head, bq=512, bk=512,
               heads=N_HEADS_DIFF):
    n = qd.shape[0]
    width = heads * LANES
    head_blk = lambda rows: pl.BlockSpec((rows, width), lambda h, i: (0, h),
                                         pipeline_mode=pl.Buffered(1))
    small = lambda a: pl.BlockSpec(a.shape, lambda h, i: (0, 0))
    return pl.pallas_call(
        functools.partial(_diff_kernel, bq=bq, bk=bk, heads=heads),
        grid=(N_HEADS_DIFF // heads, n // bq),
        in_specs=[pl.BlockSpec((bq, width), lambda h, i: (i, h)),
                  head_blk(n), head_blk(n), head_blk(META_PAD), head_blk(META_PAD),
                  small(lq1), small(lk1), small(lq2), small(lk2), small(g_head)],
        out_specs=pl.BlockSpec((bq, width), lambda h, i: (i, h)),
        out_shape=jax.ShapeDtypeStruct((n, DIFF_WIDTH), _bf16),
        scratch_shapes=[pltpu.VMEM((2 * heads, bq, LANES), _f32),
                        pltpu.VMEM((2 * heads, bq, 2 * LANES), _f32)],
        compiler_params=pltpu.CompilerParams(
            dimension_semantics=("arbitrary", "arbitrary"), vmem_limit_bytes=VMEM_LIMIT),
        name="diff_attn",
    )(qd, kdb, vdb, kmeta, vmeta, lq1, lk1, lq2, lk2, g_head)


def _suffix_matrix(n):
    j = lax.broadcasted_iota(jnp.int32, (n, n), 0)
    s = lax.broadcasted_iota(jnp.int32, (n, n), 1)
    return jnp.where(j > s, 1.0, 0.0).astype(_bf16)


def _sb_tiles(queries, keys, values, carries, mask, suffix):
    scores = [_dot_nt(q, k) for q, k in zip(queries, keys)]
    log_betas, log_keeps = [], []
    for s in scores:
        log_beta = _log2_sigmoid(s)
        log_keep = log_beta - s
        if mask is not None:
            log_keep = jnp.where(mask, log_keep, 0.0)
        log_betas.append(log_beta)
        log_keeps.append(log_keep)
    afters = [_dot(log_keep.astype(_bf16), suffix) for log_keep in log_keeps]
    weights, new_carries = [], []
    for log_beta, log_keep, after, carry in zip(log_betas, log_keeps, afters, carries):
        a = jnp.exp2(log_beta + after + carry)
        if mask is not None:
            a = jnp.where(mask, a, 0.0)
        weights.append(a.astype(_bf16))
        new_carries.append(carry + after[:, 0:1] + log_keep[:, 0:1])
    return [_dot(a, v) for a, v in zip(weights, values)], new_carries


def _sb_kernel(q_ref, k_ref, v_ref, km_ref, vm_ref, o_ref, acc_ref, carry_ref, live_ref, *, bq):
    qi = pl.program_id(0)
    lo, hi = _lane_half_masks((bq, LANES))
    zero = jnp.zeros((bq, LANES), _bf16)
    lanes, queries = [], []
    for p in range(SB_WIDTH // LANES):
        sl = slice(p * LANES, (p + 1) * LANES)
        q = q_ref[:, sl]
        lanes += [sl, sl]
        queries += [jnp.where(lo, q, zero), jnp.where(hi, q, zero)]
    heads = range(N_HEADS_SB)
    bk = 2 * bq

    def tile(k_of, v_of, mask, suffix, first):
        carries = [jnp.zeros((bq, 1), _f32) if first else carry_ref[c] for c in heads]
        pvs, carries = _sb_tiles(queries, [k_of(sl) for sl in lanes], [v_of(sl) for sl in lanes],
                                 carries, mask, suffix)
        live = carries[0]
        for c in heads:
            acc_ref[c] = pvs[c] if first else acc_ref[c] + pvs[c]
            carry_ref[c] = carries[c]
            live = jnp.maximum(live, carries[c])
        return jnp.max(live)

    def frame_tile(start, width, mask, first):
        start = pl.multiple_of(start, bq)
        return tile(lambda sl: k_ref[pl.ds(start, width), sl], lambda sl: v_ref[pl.ds(start, width), sl],
                    mask, _suffix_matrix(width), first)

    q0 = qi * bq
    start0 = jnp.maximum(q0 - bq, 0)
    row = lax.broadcasted_iota(jnp.int32, (bq, bk), 0) + q0
    col = lax.broadcasted_iota(jnp.int32, (bq, bk), 1) + start0
    live = frame_tile(start0, bk, col < row, True)

    def cond(state):
        start, live = state
        return jnp.logical_and(start >= bk, live > SB_LOG2_STOP)

    def body(state):
        start, _ = state
        return start - bk, frame_tile(start - bk, bk, None, False)

    start, live = lax.while_loop(cond, body, (start0, live))
    live_ref[0] = live

    @pl.when(jnp.logical_and(start == bq, live > SB_LOG2_STOP))
    def _():
        live_ref[0] = frame_tile(0, bq, None, False)

    @pl.when(jnp.logical_and(start < bk, live_ref[0] > SB_LOG2_STOP))
    def _():
        mcol = lax.broadcasted_iota(jnp.int32, (bq, META_PAD), 1)
        tile(lambda sl: km_ref[:, sl], lambda sl: vm_ref[:, sl], mcol < N_META,
             _suffix_matrix(META_PAD), False)

    for p in range(SB_WIDTH // LANES):
        o_ref[:, p * LANES:(p + 1) * LANES] = jnp.where(lo, acc_ref[2 * p], acc_ref[2 * p + 1]).astype(_bf16)


def _sb_attn(qs, ksb, vsb, kmeta, vmeta, bq=128):
    n = qs.shape[0]
    resident = lambda a: pl.BlockSpec(a.shape, lambda i: (0, 0), pipeline_mode=pl.Buffered(1))
    return pl.pallas_call(
        functools.partial(_sb_kernel, bq=bq),
        grid=(n // bq,),
        in_specs=[pl.BlockSpec((bq, SB_WIDTH), lambda i: (i, 0)),
                  resident(ksb), resident(vsb), resident(kmeta), resident(vmeta)],
        out_specs=pl.BlockSpec((bq, SB_WIDTH), lambda i: (i, 0)),
        out_shape=jax.ShapeDtypeStruct((n, SB_WIDTH), _bf16),
        scratch_shapes=[pltpu.VMEM((N_HEADS_SB, bq, LANES), _f32),
                        pltpu.VMEM((N_HEADS_SB, bq, 1), _f32),
                        pltpu.SMEM((1,), _f32)],
        compiler_params=pltpu.CompilerParams(
            dimension_semantics=("arbitrary",), vmem_limit_bytes=VMEM_LIMIT),
        name="sb_attn",
    )(qs, ksb, vsb, kmeta, vmeta)


def _run_phases(*phase_generators):
    pending = list(phase_generators)
    while pending:
        pending = [g for g in pending if next(g, "done") != "done"]


def _samp_kernel(*refs, ts, past):
    _run_phases(_samp_phases(*refs, ts=ts, past=past))


def _samp_phases(qd_ref, qs_ref, kdn_ref, ksn_ref, vdn_ref, vsn_ref,
                 kdm_ref, ksm_ref, vdm_ref, vsm_ref,
                 ckd_ref, cks_ref, cvd_ref, cvs_ref,
                 lq1_ref, lk1_ref, lq2_ref, lk2_ref, gh_ref,
                 od_ref, os_ref, kd_edge, ks_edge, vd_edge, vs_edge, *, ts, past):
    for edge, meta, new in ((kd_edge, kdm_ref, kdn_ref), (ks_edge, ksm_ref, ksn_ref),
                            (vd_edge, vdm_ref, vdn_ref), (vs_edge, vsm_ref, vsn_ref)):
        edge[0:N_META, :] = meta[0:N_META, :]
        edge[N_META:N_META + ts, :] = new[...]
        edge[N_META + ts:, :] = jnp.zeros((LANES - N_META - ts, DIFF_WIDTH), _bf16)

    grp = 2 * ts
    n_cols = past + LANES
    col = lax.broadcasted_iota(jnp.int32, (grp, n_cols), 1)
    ecol = col - past
    pos_k = jnp.where(ecol < 0, col + N_META, jnp.where(ecol < N_META, ecol, ecol + past))
    pos_q = (lax.broadcasted_iota(jnp.int32, (grp, n_cols), 0) & (ts - 1)) + N_META + past
    valid = ecol < N_META + ts
    chunk = lambda p: jnp.where(p < N_META, -1, (p - N_META) >> CHUNK_SHIFT)
    mask_chunk = jnp.logical_and(valid, chunk(pos_k) <= chunk(pos_q))
    mask_strict = jnp.logical_and(valid, pos_k < pos_q)

    lo, hi = _lane_half_masks((ts, LANES))
    zero = jnp.zeros((ts, LANES), _bf16)
    blocks = [slice(j * LANES, (j + 1) * LANES) for j in range(DIFF_WIDTH // LANES)]

    def stacked(q):
        return jnp.concatenate([jnp.where(lo, q, zero), jnp.where(hi, q, zero)], axis=0)

    def scores(q_ref, cache_ref, edge):
        out = []
        for sl in blocks:
            q = stacked(q_ref[:, sl])
            out.append(jnp.concatenate([_dot(q, cache_ref[0, sl, :].astype(_bf16)),
                                        _dot_nt(q, edge[:, sl])], axis=1))
        return out

    diff_scores = scores(qd_ref, ckd_ref, kd_edge)
    sb_scores = scores(qs_ref, cks_ref, ks_edge)
    yield

    lam = _lambda(lq1_ref[...], lk1_ref[...], lq2_ref[...], lk2_ref[...])
    probs, sums = [], []
    for s in diff_scores:
        s = jnp.where(mask_chunk, s, NEG_INF)
        p = jnp.exp2(s - jnp.max(s, axis=-1, keepdims=True))
        probs.append(p.astype(_bf16))
        sums.append(jnp.sum(p, axis=-1, keepdims=True))
    for h, (sl, p, l) in enumerate(zip(blocks, probs, sums)):
        v_cache = cvd_ref[0, pl.ds(h, past, stride=N_HEADS_DIFF), :].astype(_bf16)
        pv = _dot(p[:, :past], v_cache) + _dot(p[:, past:], vd_edge[:, sl])
        out = _diff_finish(pv[:ts], l[:ts], pv[ts:], l[ts:], lam, gh_ref[...])
        od_ref[:, sl] = out.astype(_bf16)
    yield

    n_tiles = past // LANES
    tiles = [slice(t * LANES, (t + 1) * LANES) for t in range(n_tiles)]
    edge_cols = slice(past, n_cols)
    log_betas, log_keeps, cache_pieces, edge_pieces = [], [], [], []
    for s in sb_scores:
        log_beta = _log2_sigmoid(s)
        log_keep = jnp.where(mask_strict, log_beta - s, 0.0)
        keep_hi = log_keep.astype(_bf16)
        keep_lo = (log_keep - keep_hi.astype(_f32)).astype(_bf16)
        log_betas.append(log_beta)
        log_keeps.append(log_keep)
        cache_pieces += [keep_hi[:, t] for t in tiles] + [keep_lo[:, t] for t in tiles]
        edge_pieces += [keep_hi[:, edge_cols], keep_lo[:, edge_cols]]
    after_cache = _dot(jnp.concatenate(cache_pieces, axis=0), _suffix_matrix(LANES))
    ej = lax.broadcasted_iota(jnp.int32, (LANES, LANES), 0)
    es = lax.broadcasted_iota(jnp.int32, (LANES, LANES), 1)
    edge_suffix = jnp.where(jnp.logical_and(ej > es, (ej < N_META) == (es < N_META)), 1.0, 0.0)
    after_edge = _dot(jnp.concatenate(edge_pieces, axis=0), edge_suffix.astype(_bf16))
    yield
    is_meta = lax.broadcasted_iota(jnp.int32, (grp, LANES), 1) < N_META
    for j, sl in enumerate(blocks):
        base = j * 2 * n_tiles
        after = [after_cache[(base + t) * grp:(base + t + 1) * grp]
                 + after_cache[(base + n_tiles + t) * grp:(base + n_tiles + t + 1) * grp]
                 for t in range(n_tiles)]
        edge_after = (after_edge[2 * j * grp:(2 * j + 1) * grp]
                      + after_edge[(2 * j + 1) * grp:(2 * j + 2) * grp])
        keep_edge = log_keeps[j][:, edge_cols]
        carry = edge_after[:, N_META:N_META + 1] + keep_edge[:, N_META:N_META + 1]
        shifted = [None] * n_tiles
        for t in reversed(range(n_tiles)):
            shifted[t] = after[t] + carry
            carry = carry + after[t][:, 0:1] + log_keeps[j][:, tiles[t]][:, 0:1]
        shifted.append(edge_after + jnp.where(is_meta, carry, 0.0))
        a = jnp.exp2(log_betas[j] + jnp.concatenate(shifted, axis=1))
        a = jnp.where(mask_strict, a, 0.0).astype(_bf16)
        pv = _dot_nt(a[:, :past], cvs_ref[0, sl, :].astype(_bf16)) + _dot(a[:, past:], vs_edge[:, sl])
        os_ref[:, sl] = jnp.where(lo, pv[:ts], pv[ts:]).astype(_bf16)


N_SAMP_INPUTS = 19


def _samp_call_parts(args, ts):
    assert len(args) == N_SAMP_INPUTS
    nb, _, past = args[10].shape
    assert past % LANES == 0 and N_META + ts <= LANES
    row_blk = pl.BlockSpec((ts, DIFF_WIDTH), lambda b: (b, 0))
    per_batch = lambda a: pl.BlockSpec((1,) + a.shape[1:], lambda b: (b, 0, 0))
    small = lambda a: pl.BlockSpec(a.shape, lambda b: (0, 0))
    out = jax.ShapeDtypeStruct((nb * ts, DIFF_WIDTH), _bf16)
    in_specs = ([row_blk] * 6 + [small(a) for a in args[6:10]] + [per_batch(a) for a in args[10:14]]
                + [small(a) for a in args[14:]])
    scratch = [pltpu.VMEM((LANES, DIFF_WIDTH), _bf16)] * 4
    return dict(ts=ts, past=past), nb, in_specs, [row_blk, row_blk], [out, out], scratch


def _samp_attn(args, ts):
    kwargs, nb, in_specs, out_specs, out_shape, scratch = _samp_call_parts(args, ts)
    return pl.pallas_call(
        functools.partial(_samp_kernel, **kwargs),
        grid=(nb,),
        in_specs=in_specs, out_specs=out_specs, out_shape=out_shape, scratch_shapes=scratch,
        compiler_params=pltpu.CompilerParams(
            dimension_semantics=("arbitrary",), vmem_limit_bytes=VMEM_LIMIT),
        name="samp_attn",
    )(*args)


def _mlp_kernel(*refs, tf):
    _run_phases(_mlp_phases(*refs, tf=tf))


def _mlp_phases(x_ref, md_ref, ms_ref, wo_ref, gm_ref, wu_ref, wd_ref, gf_ref, o_ref, h2_ref, *, tf):
    x1 = (x_ref[...] + _dot(md_ref[...], wo_ref[:DIFF_WIDTH, :])
          + _dot(ms_ref[...], wo_ref[DIFF_WIDTH:, :]))
    h2_ref[...] = _rms(x1, gm_ref[...]).astype(_bf16)
    o_ref[...] = x1
    yield
    for f in range(D_FF // tf):
        u = jnp.maximum(_dot(h2_ref[...], wu_ref[:, f * tf:(f + 1) * tf]), 0.0)
        o_ref[...] += _dot((u * u).astype(_bf16), wd_ref[f * tf:(f + 1) * tf, :])
        yield
    o_ref[...] = _rms(o_ref[...], gf_ref[...])


N_MLP_INPUTS = 8
MLP_ROWS = 512
MLP_FF_CHUNK = 1024


def _mlp_call_parts(args):
    assert len(args) == N_MLP_INPUTS
    x, _, _, wo, g_mlp, wu, wd, g_final = args
    rows, tm = x.shape[0], MLP_ROWS
    resident = lambda a: pl.BlockSpec(a.shape, lambda i: (0, 0), pipeline_mode=pl.Buffered(1))
    small = lambda a: pl.BlockSpec(a.shape, lambda i: (0, 0))
    row_blk = lambda width: pl.BlockSpec((tm, width), lambda i: (i, 0))
    in_specs = [row_blk(D_MODEL), row_blk(DIFF_WIDTH), row_blk(SB_WIDTH),
                resident(wo), small(g_mlp), resident(wu), resident(wd), small(g_final)]
    return (rows // tm, in_specs, row_blk(D_MODEL), jax.ShapeDtypeStruct((rows, D_MODEL), _f32),
            [pltpu.VMEM((tm, D_MODEL), _bf16)])


def _mlp(*args):
    steps, in_specs, out_spec, out_shape, scratch = _mlp_call_parts(args)
    return pl.pallas_call(
        functools.partial(_mlp_kernel, tf=MLP_FF_CHUNK),
        grid=(steps,),
        in_specs=in_specs, out_specs=out_spec, out_shape=out_shape, scratch_shapes=scratch,
        compiler_params=pltpu.CompilerParams(
            dimension_semantics=("arbitrary",), vmem_limit_bytes=VMEM_LIMIT),
        name="mlp",
    )(*args)


def _mlp_samp_kernel(*refs, samp_kwargs):
    n_in = N_MLP_INPUTS + N_SAMP_INPUTS
    mlp_in, samp_in = refs[:N_MLP_INPUTS], refs[N_MLP_INPUTS:n_in]
    y_ref, od_ref, os_ref, h2_ref = refs[n_in:n_in + 4]
    edges = refs[n_in + 4:]
    _run_phases(_samp_phases(*samp_in, od_ref, os_ref, *edges, **samp_kwargs),
                _mlp_phases(*mlp_in, y_ref, h2_ref, tf=MLP_FF_CHUNK))


def _mlp_and_samp_attn(mlp_args, samp_args, ts):
    steps, m_in, m_out, m_shape, m_scratch = _mlp_call_parts(mlp_args)
    kwargs, nb, s_in, s_out, s_shape, s_scratch = _samp_call_parts(samp_args, ts)
    if steps != nb:
        return (_mlp(*mlp_args),) + tuple(_samp_attn(samp_args, ts))
    return pl.pallas_call(
        functools.partial(_mlp_samp_kernel, samp_kwargs=kwargs),
        grid=(steps,),
        in_specs=m_in + s_in, out_specs=[m_out] + s_out, out_shape=[m_shape] + s_shape,
        scratch_shapes=m_scratch + s_scratch,
        compiler_params=pltpu.CompilerParams(
            dimension_semantics=("arbitrary",), vmem_limit_bytes=VMEM_LIMIT),
        name="mlp_samp_attn",
    )(*mlp_args, *samp_args)


def kernel(x_prompt, x_sample, cache_diff_k, cache_diff_v, cache_sb_k, cache_sb_v, meta_tokens, g_mix, w_in, lambda_q1, lambda_k1, lambda_q2, lambda_k2, g_diff_head, w_out, g_mlp, w_up, w_down, g_final):
    bp, tp, d = x_prompt.shape
    bs, ts, _ = x_sample.shape
    past = cache_diff_k.shape[2]
    assert bp == 1 and g_mix.shape[0] == 1, "single prompt sequence, depth 1"

    w_in_b = w_in[0].astype(_bf16)
    w_out_b = w_out[0].astype(_bf16)
    w_up_b = w_up[0].astype(_bf16)
    w_down_b = w_down[0].astype(_bf16)
    g_final2 = g_final[None, :]
    lams = (lambda_q1, lambda_k1, lambda_q2, lambda_k2)

    x_ms = jnp.concatenate([meta_tokens.astype(x_sample.dtype), x_sample.reshape(bs * ts, d)], axis=0)
    pos_ms = jnp.concatenate([jnp.arange(N_META), jnp.tile(N_META + past + jnp.arange(ts), bs)])
    cos_s, sin_s = _rope_tables(pos_ms)
    s_out = _proj(x_ms, g_mix, w_in_b, cos_s, sin_s)
    meta_f32 = [a[:N_META] for a in s_out[2:6]]
    samp_f32 = [a[N_META:] for a in s_out[2:6]]
    pad_meta = lambda a: jnp.pad(a[:N_META], ((0, META_PAD - N_META), (0, 0)))
    kdm, ksm, vdm, vsm = [pad_meta(a) for a in s_out[6:10]]
    qd_s, qs_s = s_out[0][N_META:], s_out[1][N_META:]
    kdn, ksn, vdn, vsn = [a[N_META:] for a in s_out[6:10]]

    p_out = _proj_prompt(x_prompt[0], g_mix, w_in_b, N_META, meta_f32, tm=512)
    qd, qs, kd, ks, vd, vs, kdb, ksb, vdb, vsb = p_out

    mixed_d = _diff_attn(qd, kdb, vdb, kdm, vdm, *lams, g_diff_head)
    mixed_s = _sb_attn(qs, ksb, vsb, ksm, vsm)

    feat_major = lambda c: jnp.transpose(c[0].reshape(bs, past, DIFF_WIDTH), (0, 2, 1))
    mlp_weights = (w_out_b, g_mlp, w_up_b, w_down_b, g_final2)
    samp_args = (qd_s, qs_s, kdn, ksn, vdn, vsn, kdm, ksm, vdm, vsm,
                 feat_major(cache_diff_k), feat_major(cache_sb_k),
                 cache_diff_v[0].reshape(bs, past * N_HEADS_DIFF, 2 * HEAD_DIM),
                 feat_major(cache_sb_v), *lams, g_diff_head)
    y_prompt, smix_d, smix_s = _mlp_and_samp_attn((x_prompt[0], mixed_d, mixed_s) + mlp_weights,
                                                  samp_args, ts)
    y_sample = _mlp(x_sample.reshape(bs * ts, d), smix_d, smix_s, *mlp_weights)

    def prompt_cache(a, tail):
        return a.reshape((1, bp, N_META + tp) + tail)

    def sample_cache(a, tail):
        return a.reshape((1, bs, ts) + tail)

    dk_tail, dv_tail = (N_HEADS_DIFF, 2, HEAD_DIM), (N_HEADS_DIFF, 2 * HEAD_DIM)
    sb_tail = (N_HEADS_SB, HEAD_DIM)
    return (y_prompt[None], y_sample.reshape(bs, ts, d),
            prompt_cache(kd, dk_tail), prompt_cache(vd, dv_tail),
            prompt_cache(ks, sb_tail), prompt_cache(vs, sb_tail),
            sample_cache(samp_f32[0], dk_tail), sample_cache(samp_f32[2], dv_tail),
            sample_cache(samp_f32[1], sb_tail), sample_cache(samp_f32[3], sb_tail))
```

```python
import functools
import math

import jax
import jax.numpy as jnp
from jax import lax
from jax.experimental import pallas as pl
from jax.experimental.pallas import tpu as pltpu

D_MODEL = 1024
CHUNK = 64
CHUNK_SHIFT = 6
N_META = 16
HEAD_DIM = 64
N_HEADS_DIFF = 4
N_HEADS_SB = 8
DIFF_WIDTH = N_HEADS_DIFF * 2 * HEAD_DIM
SB_WIDTH = N_HEADS_SB * HEAD_DIM
MIX_WIDTH = DIFF_WIDTH + SB_WIDTH
D_IN = 3 * MIX_WIDTH
D_FF = 4 * D_MODEL
ROPE_THETA = 10000.0
NORM_EPS = 1e-6
NEG_INF = -1e30
LAMBDA_INIT = 0.8 - 0.6 * math.exp(-0.3 * 0)
LOG2_E = math.log2(math.e)

LANES = 128
META_PAD = 128
SB_LOG2_STOP = -44.0
SCORE_LEAD = 3

VMEM_LIMIT = 56 * 1024 * 1024

_f32 = jnp.float32
_bf16 = jnp.bfloat16


def _dot(a, b):
    return jnp.dot(a, b, preferred_element_type=_f32)


def _dot_nt(a, b):
    return lax.dot_general(a, b, (((1,), (1,)), ((), ())), preferred_element_type=_f32)


def _rms(x, g):
    return x * lax.rsqrt(jnp.mean(x * x, axis=-1, keepdims=True) + NORM_EPS) * g


def _log2_sigmoid(s2):
    return jnp.minimum(s2, 0.0) - jnp.log2(1.0 + jnp.exp2(-jnp.abs(s2)))


def _first_half_lanes(shape):
    lane = lax.broadcasted_iota(jnp.int32, shape, 1)
    return (lane & (HEAD_DIM - 1)) < (HEAD_DIM // 2)


def _proj_math(x_ref, g_ref, w_ref, cos, sin):
    h = _rms(x_ref[...], g_ref[...]).astype(_bf16)
    first_half = _first_half_lanes(cos.shape)

    def rope(z):
        parts = []
        for j in range(z.shape[1] // LANES):
            zj = z[:, j * LANES:(j + 1) * LANES]
            swapped = jnp.where(first_half,
                                pltpu.roll(zj, LANES - HEAD_DIM // 2, 1),
                                pltpu.roll(zj, HEAD_DIM // 2, 1))
            parts.append(zj * cos + swapped * sin)
        return jnp.concatenate(parts, axis=1)

    def cols(c):
        return _dot(h, w_ref[:, c * DIFF_WIDTH:(c + 1) * DIFF_WIDTH])

    scale = HEAD_DIM ** -0.5
    scale = scale * LOG2_E
    return (rope(cols(0)) * scale, cols(1) * scale, rope(cols(2)), cols(3), cols(4), cols(5))


def _proj_kernel(x_ref, g_ref, w_ref, cos_ref, sin_ref,
                 qd_ref, qs_ref, kd_ref, ks_ref, vd_ref, vs_ref,
                 kdb_ref, ksb_ref, vdb_ref, vsb_ref):
    qd, qs, kd, ks, vd, vs = _proj_math(x_ref, g_ref, w_ref, cos_ref[...], sin_ref[...])
    qd_ref[...] = qd.astype(_bf16)
    qs_ref[...] = qs.astype(_bf16)
    for f32_ref, b16_ref, val in ((kd_ref, kdb_ref, kd), (ks_ref, ksb_ref, ks),
                                  (vd_ref, vdb_ref, vd), (vs_ref, vsb_ref, vs)):
        f32_ref[...] = val
        b16_ref[...] = val.astype(_bf16)


def _proj(x, g, w_bf16, cos, sin):
    rows = x.shape[0]
    full = lambda a: pl.BlockSpec(a.shape, lambda i: (0, 0))
    blk = lambda width: pl.BlockSpec((rows, width), lambda i: (0, 0))
    f32_out = jax.ShapeDtypeStruct((rows, DIFF_WIDTH), _f32)
    b16_out = jax.ShapeDtypeStruct((rows, DIFF_WIDTH), _bf16)
    return pl.pallas_call(
        _proj_kernel,
        grid=(1,),
        in_specs=[full(x), full(g), full(w_bf16), full(cos), full(sin)],
        out_specs=[blk(DIFF_WIDTH)] * 10,
        out_shape=[b16_out, b16_out, f32_out, f32_out, f32_out, f32_out,
                   b16_out, b16_out, b16_out, b16_out],
        compiler_params=pltpu.CompilerParams(
            dimension_semantics=("arbitrary",), vmem_limit_bytes=VMEM_LIMIT),
        name="proj",
    )(x, g, w_bf16, cos, sin)


def _proj_prompt_kernel(x_ref, g_ref, w_ref, cos_row_ref, sin_row_ref, cos_blk_ref, sin_blk_ref,
                        mkd_ref, mks_ref, mvd_ref, mvs_ref,
                        qd_ref, qs_ref, kd_ref, ks_ref, vd_ref, vs_ref,
                        kdb_ref, ksb_ref, vdb_ref, vsb_ref, tail_ref, *, tm):
    i = pl.program_id(0)
    last = pl.num_programs(0) - 1
    row_outs = ((0, kd_ref), (1, ks_ref), (3, vs_ref))

    def put_vd(token0, rows):
        for h in range(N_HEADS_DIFF):
            vd_ref[pl.ds(token0 * N_HEADS_DIFF + h, rows.shape[0], stride=N_HEADS_DIFF), :] = (
                rows[:, h * LANES:(h + 1) * LANES])

    @pl.when(i == 0)
    def _():
        for j, meta in enumerate((mkd_ref, mks_ref, mvd_ref, mvs_ref)):
            tail_ref[j] = meta[...]

    @pl.when(i < last)
    def _():
        cos_r, sin_r, cos_b, sin_b = cos_row_ref[...], sin_row_ref[...], cos_blk_ref[0], sin_blk_ref[0]
        cos = cos_b * cos_r - sin_b * sin_r
        sin = sin_b * cos_r + cos_b * sin_r
        sin = jnp.where(_first_half_lanes(sin.shape), -sin, sin)
        qd, qs, kd, ks, vd, vs = _proj_math(x_ref, g_ref, w_ref, cos, sin)
        qd_ref[...] = qd.astype(_bf16)
        qs_ref[...] = qs.astype(_bf16)
        for b16_ref, val in ((kdb_ref, kd), (ksb_ref, ks), (vdb_ref, vd), (vsb_ref, vs)):
            b16_ref[...] = val.astype(_bf16)
        for (j, out), val in zip(row_outs, (kd, ks, vs)):
            out[:N_META, :] = tail_ref[j]
            out[N_META:, :] = val[:tm - N_META]
            tail_ref[j] = val[tm - N_META:]
        put_vd(0, tail_ref[2])
        put_vd(N_META, vd[:tm - N_META])
        tail_ref[2] = vd[tm - N_META:]

    @pl.when(i == last)
    def _():
        for j, out in row_outs:
            out[:N_META, :] = tail_ref[j]
        put_vd(0, tail_ref[2])


def _proj_prompt(x, g, w_bf16, pos0, meta_kv, tm):
    rows = x.shape[0]
    nblk = rows // tm
    cos_row, sin_row = _angle_tables(jnp.arange(tm))
    cos_blk, sin_blk = [t[:, None, :] for t in _angle_tables(pos0 + tm * jnp.arange(nblk))]
    frame_blk = lambda width: pl.BlockSpec((tm, width), lambda i: (jnp.minimum(i, nblk - 1), 0))
    blk_angle = pl.BlockSpec((1, 1, LANES), lambda i: (jnp.minimum(i, nblk - 1), 0, 0))
    full = lambda a: pl.BlockSpec(a.shape, lambda i: (0, 0))
    f32_out = jax.ShapeDtypeStruct((N_META + rows, DIFF_WIDTH), _f32)
    vd_out = jax.ShapeDtypeStruct(((N_META + rows) * N_HEADS_DIFF, LANES), _f32)
    b16_out = jax.ShapeDtypeStruct((rows, DIFF_WIDTH), _bf16)
    stream_blk = pl.BlockSpec((tm, DIFF_WIDTH), lambda i: (i, 0))
    return pl.pallas_call(
        functools.partial(_proj_prompt_kernel, tm=tm),
        grid=(nblk + 1,),
        in_specs=[frame_blk(D_MODEL), full(g), full(w_bf16), full(cos_row), full(sin_row),
                  blk_angle, blk_angle] + [full(m) for m in meta_kv],
        out_specs=[frame_blk(DIFF_WIDTH)] * 2
                  + [stream_blk, stream_blk,
                     pl.BlockSpec((tm * N_HEADS_DIFF, LANES), lambda i: (i, 0)), stream_blk]
                  + [frame_blk(DIFF_WIDTH)] * 4,
        out_shape=[b16_out, b16_out, f32_out, f32_out, vd_out, f32_out,
                   b16_out, b16_out, b16_out, b16_out],
        scratch_shapes=[pltpu.VMEM((4, N_META, DIFF_WIDTH), _f32)],
        compiler_params=pltpu.CompilerParams(
            dimension_semantics=("arbitrary",), vmem_limit_bytes=VMEM_LIMIT),
        name="proj_prompt",
    )(x, g, w_bf16, cos_row, sin_row, cos_blk, sin_blk, *meta_kv)


def _angle_tables(pos):
    half = HEAD_DIM // 2
    inv_freq = ROPE_THETA ** (-jnp.arange(half, dtype=_f32) / half)
    ang = pos.astype(_f32)[:, None] * inv_freq[None, :]
    reps = LANES // half
    return jnp.tile(jnp.cos(ang), (1, reps)), jnp.tile(jnp.sin(ang), (1, reps))


def _rope_tables(pos):
    cos, sin = _angle_tables(pos)
    return cos, jnp.where(_first_half_lanes(sin.shape), -sin, sin)


def _lane_half_masks(shape):
    lane = lax.broadcasted_iota(jnp.int32, shape, 1)
    return lane < HEAD_DIM, lane >= HEAD_DIM


def _lambda(lq1, lk1, lq2, lk2):
    return (jnp.exp(jnp.sum(lq1 * lk1, axis=-1, keepdims=True))
            - jnp.exp(jnp.sum(lq2 * lk2, axis=-1, keepdims=True)) + LAMBDA_INIT)


def _diff_finish(o0, l0, o1, l1, lam, g_head):
    o = o0 / l0 - lam * (o1 / l1)
    return _rms(o, g_head) * (1.0 - LAMBDA_INIT)


def _diff_kernel(q_ref, k_ref, v_ref, km_ref, vm_ref, lq1_ref, lk1_ref, lq2_ref, lk2_ref, gh_ref,
                 o_ref, m_ref, acc_ref, *, bq, bk, heads):
    qi = pl.program_id(1)
    lo, hi = _lane_half_masks((bq, LANES))
    zero = jnp.zeros((bq, LANES), _bf16)
    chains = []
    for a in range(heads):
        sl = slice(a * LANES, (a + 1) * LANES)
        q = q_ref[:, sl]
        chains.append((2 * a, sl, jnp.where(lo, q, zero)))
        chains.append((2 * a + 1, sl, jnp.where(hi, q, zero)))

    def softmax_pv(c, s, v, mask, first):
        if mask is not None:
            s = jnp.where(mask, s, NEG_INF)
        m_cur = jnp.max(s, axis=-1, keepdims=True)
        m_new = jnp.broadcast_to(m_cur, (bq, LANES)) if first else jnp.maximum(m_ref[c], m_cur)
        p = jnp.exp2((s - jnp.tile(m_new, (1, s.shape[1] // LANES))).astype(_bf16))
        pv = _dot(p, jnp.concatenate([v, jnp.ones_like(v)], axis=1))
        if first:
            acc_ref[c] = pv
        else:
            alpha = jnp.exp2(m_ref[c] - m_new)
            acc_ref[c] = jnp.tile(alpha, (1, 2)) * acc_ref[c] + pv
        m_ref[c] = m_new

    def step(k_of, v_of, mask, first):
        scores = {}
        for i in range(len(chains) + SCORE_LEAD):
            if i < len(chains):
                _, sl, qz = chains[i]
                scores[i] = _dot_nt(qz, k_of(sl))
            if i >= SCORE_LEAD:
                c, sl, _ = chains[i - SCORE_LEAD]
                softmax_pv(c, scores.pop(i - SCORE_LEAD), v_of(sl), mask, first)

    q0 = qi * bq
    n_full = (q0 + CHUNK) // bk

    start = pl.multiple_of(n_full * bk, bk)
    row = lax.broadcasted_iota(jnp.int32, (bq, bk + META_PAD), 0) + q0
    col = lax.broadcasted_iota(jnp.int32, (bq, bk + META_PAD), 1)
    key_chunk = jnp.where(col < bk, (col + start) >> CHUNK_SHIFT,
                          jnp.where(col < bk + N_META, -1, jnp.iinfo(jnp.int32).max))
    mask = key_chunk <= (row >> CHUNK_SHIFT)
    step(lambda sl: jnp.concatenate([k_ref[pl.ds(start, bk), sl], km_ref[:, sl]], axis=0),
         lambda sl: jnp.concatenate([v_ref[pl.ds(start, bk), sl], vm_ref[:, sl]], axis=0), mask, True)

    def full_keys(start, width):
        start = pl.multiple_of(start, bk)
        step(lambda sl: k_ref[pl.ds(start, width), sl], lambda sl: v_ref[pl.ds(start, width), sl],
             None, False)

    def tile_pair(j, carry):
        full_keys(j * 2 * bk, 2 * bk)
        return carry

    lax.fori_loop(0, n_full // 2, tile_pair, 0)

    @pl.when(n_full % 2 == 1)
    def _():
        full_keys((n_full - 1) * bk, bk)

    lam = _lambda(lq1_ref[...], lk1_ref[...], lq2_ref[...], lk2_ref[...])
    for a in range(heads):
        acc0, acc1 = acc_ref[2 * a], acc_ref[2 * a + 1]
        out = _diff_finish(acc0[:, :LANES], acc0[:, LANES:], acc1[:, :LANES], acc1[:, LANES:],
                           lam, gh_ref[...])
        o_ref[:, a * LANES:(a + 1) * LANES] = out.astype(_bf16)


def _diff_attn(qd, kdb, vdb, kmeta, vmeta, lq1, lk1, lq2, lk2, g_head, bq=512, bk=512,
               heads=N_HEADS_DIFF):
    n = qd.shape[0]
    width = heads * LANES
    head_blk = lambda rows: pl.BlockSpec((rows, width), lambda h, i: (0, h),
                                         pipeline_mode=pl.Buffered(1))
    small = lambda a: pl.BlockSpec(a.shape, lambda h, i: (0, 0))
    return pl.pallas_call(
        functools.partial(_diff_kernel, bq=bq, bk=bk, heads=heads),
        grid=(N_HEADS_DIFF // heads, n // bq),
        in_specs=[pl.BlockSpec((bq, width), lambda h, i: (i, h)),
                  head_blk(n), head_blk(n), head_blk(META_PAD), head_blk(META_PAD),
                  small(lq1), small(lk1), small(lq2), small(lk2), small(g_head)],
        out_specs=pl.BlockSpec((bq, width), lambda h, i: (i, h)),
        out_shape=jax.ShapeDtypeStruct((n, DIFF_WIDTH), _bf16),
        scratch_shapes=[pltpu.VMEM((2 * heads, bq, LANES), _f32),
                        pltpu.VMEM((2 * heads, bq, 2 * LANES), _f32)],
        compiler_params=pltpu.CompilerParams(
            dimension_semantics=("arbitrary", "arbitrary"), vmem_limit_bytes=VMEM_LIMIT),
        name="diff_attn",
    )(qd, kdb, vdb, kmeta, vmeta, lq1, lk1, lq2, lk2, g_head)


def _suffix_matrix(n):
    j = lax.broadcasted_iota(jnp.int32, (n, n), 0)
    s = lax.broadcasted_iota(jnp.int32, (n, n), 1)
    return jnp.where(j > s, 1.0, 0.0).astype(_bf16)


def _sb_tile_phases(queries, keys, values, carries, mask, suffix, out):
    masks = mask if isinstance(mask, (list, tuple)) else [mask] * len(queries)
    scores = [_dot_nt(q, k) for q, k in zip(queries, keys)]
    yield
    log_betas, log_keeps = [], []
    for s, m in zip(scores, masks):
        log_beta = _log2_sigmoid(s)
        log_keep = log_beta - s
        if m is not None:
            log_keep = jnp.where(m, log_keep, 0.0)
        log_betas.append(log_beta)
        log_keeps.append(log_keep)
    afters = [_dot(log_keep.astype(_bf16), suffix) for log_keep in log_keeps]
    yield
    weights, new_carries = [], []
    for log_beta, log_keep, after, carry, m in zip(log_betas, log_keeps, afters, carries, masks):
        a = jnp.exp2(log_beta + after + carry)
        if m is not None:
            a = jnp.where(m, a, 0.0)
        weights.append(a.astype(_bf16))
        new_carries.append(carry + after[:, 0:1] + log_keep[:, 0:1])
    out.append(([_dot(a, v) for a, v in zip(weights, values)], new_carries))


def _sb_kernel(q_ref, k_ref, v_ref, km_ref, vm_ref, o_ref, acc_ref, carry_ref, live_ref, *, bq):
    qi = pl.program_id(0)
    lo, hi = _lane_half_masks((bq, LANES))
    zero = jnp.zeros((bq, LANES), _bf16)
    pairs = [slice(p * LANES, (p + 1) * LANES) for p in range(SB_WIDTH // LANES)]
    lanes = [sl for sl in pairs for _ in range(2)]
    heads = range(N_HEADS_SB)
    bk = 2 * bq
    queries = []
    for sl in pairs:
        q = q_ref[:, sl]
        queries += [jnp.where(lo, q, zero), jnp.where(hi, q, zero)]

    def older_tile(keys, values, mask, width):
        result = []
        _run_phases(_sb_tile_phases(queries, keys, values, [carry_ref[c] for c in heads], mask,
                                    _suffix_matrix(width), result))
        pvs, carries = result[0]
        live = carries[0]
        for c in heads:
            acc_ref[c] += pvs[c]
            carry_ref[c] = carries[c]
            live = jnp.maximum(live, carries[c])
        return jnp.max(live)

    def frame_tile(start, width):
        start = pl.multiple_of(start, bq)
        return older_tile([k_ref[pl.ds(start, width), sl] for sl in lanes],
                          [v_ref[pl.ds(start, width), sl] for sl in lanes], None, width)

    def half_tiles(starts, masks, first):
        qs2, ks2, vs2, carries = [], [], [], []
        for h, rows in enumerate((slice(0, half), slice(half, bq))):
            start = pl.multiple_of(starts[h], half)
            for c in heads:
                qs2.append(queries[c][rows])
                ks2.append(k_ref[pl.ds(start, bq), lanes[c]])
                vs2.append(v_ref[pl.ds(start, bq), lanes[c]])
                carries.append(jnp.zeros((half, 1), _f32) if first else carry_ref[c, rows])
        result = []
        _run_phases(_sb_tile_phases(qs2, ks2, vs2, carries, [m for m in masks for _ in heads],
                                    _suffix_matrix(bq), result))
        pvs, carries = result[0]
        live = carries[0]
        for h, rows in enumerate((slice(0, half), slice(half, bq))):
            for c in heads:
                i = h * N_HEADS_SB + c
                acc_ref[c, rows] = pvs[i] if first else acc_ref[c, rows] + pvs[i]
                carry_ref[c, rows] = carries[i]
                live = jnp.maximum(live, carries[i])
        return jnp.max(live)

    q0 = qi * bq
    half = bq // 2
    start_a = jnp.maximum(q0 - half, 0)
    row = lax.broadcasted_iota(jnp.int32, (half, bq), 0) + q0
    col = lax.broadcasted_iota(jnp.int32, (half, bq), 1)
    live_ref[0] = half_tiles((start_a, q0), (col + start_a < row, col + q0 < row + half), True)

    @pl.when(jnp.logical_and(q0 > 0, live_ref[0] > SB_LOG2_STOP))
    def _():
        live_ref[0] = half_tiles((q0 - bq, q0 - bq), (col < half, None), False)

    start0 = jnp.maximum(q0 - bq, 0)

    def cond(state):
        start, live = state
        return jnp.logical_and(start >= bk, live > SB_LOG2_STOP)

    def body(state):
        start, _ = state
        return start - bk, frame_tile(start - bk, bk)

    start, live = lax.while_loop(cond, body, (start0, live_ref[0]))
    live_ref[0] = live

    @pl.when(jnp.logical_and(start == bq, live > SB_LOG2_STOP))
    def _():
        live_ref[0] = frame_tile(0, bq)

    @pl.when(jnp.logical_and(start < bk, live_ref[0] > SB_LOG2_STOP))
    def _():
        mcol = lax.broadcasted_iota(jnp.int32, (bq, META_PAD), 1)
        older_tile([km_ref[:, sl] for sl in lanes], [vm_ref[:, sl] for sl in lanes], mcol < N_META,
                   META_PAD)

    for p, sl in enumerate(pairs):
        o_ref[:, sl] = jnp.where(lo, acc_ref[2 * p], acc_ref[2 * p + 1]).astype(_bf16)


def _sb_attn(qs, ksb, vsb, kmeta, vmeta, bq=128):
    n = qs.shape[0]
    resident = lambda a: pl.BlockSpec(a.shape, lambda i: (0, 0), pipeline_mode=pl.Buffered(1))
    return pl.pallas_call(
        functools.partial(_sb_kernel, bq=bq),
        grid=(n // bq,),
        in_specs=[pl.BlockSpec((bq, SB_WIDTH), lambda i: (i, 0)),
                  resident(ksb), resident(vsb), resident(kmeta), resident(vmeta)],
        out_specs=pl.BlockSpec((bq, SB_WIDTH), lambda i: (i, 0)),
        out_shape=jax.ShapeDtypeStruct((n, SB_WIDTH), _bf16),
        scratch_shapes=[pltpu.VMEM((N_HEADS_SB, bq, LANES), _f32),
                        pltpu.VMEM((N_HEADS_SB, bq, 1), _f32),
                        pltpu.SMEM((1,), _f32)],
        compiler_params=pltpu.CompilerParams(
            dimension_semantics=("arbitrary",), vmem_limit_bytes=VMEM_LIMIT),
        name="sb_attn",
    )(qs, ksb, vsb, kmeta, vmeta)


def _run_phases(*phase_generators):
    pending = list(phase_generators)
    while pending:
        pending = [g for g in pending if next(g, "done") != "done"]


def _samp_kernel(*refs, ts, past):
    _run_phases(_samp_phases(*refs, ts=ts, past=past))


def _samp_phases(qd_ref, qs_ref, kdn_ref, ksn_ref, vdn_ref, vsn_ref,
                 kdm_ref, ksm_ref, vdm_ref, vsm_ref,
                 ckd_ref, cks_ref, cvd_ref, cvs_ref,
                 lq1_ref, lk1_ref, lq2_ref, lk2_ref, gh_ref,
                 od_ref, os_ref, kd_edge, ks_edge, vd_edge, vs_edge, *, ts, past):
    for edge, meta, new in ((kd_edge, kdm_ref, kdn_ref), (ks_edge, ksm_ref, ksn_ref),
                            (vd_edge, vdm_ref, vdn_ref), (vs_edge, vsm_ref, vsn_ref)):
        edge[0:N_META, :] = meta[0:N_META, :]
        edge[N_META:N_META + ts, :] = new[...]
        edge[N_META + ts:, :] = jnp.zeros((LANES - N_META - ts, DIFF_WIDTH), _bf16)

    grp = 2 * ts
    n_cols = past + LANES
    col = lax.broadcasted_iota(jnp.int32, (grp, n_cols), 1)
    ecol = col - past
    pos_k = jnp.where(ecol < 0, col + N_META, jnp.where(ecol < N_META, ecol, ecol + past))
    pos_q = (lax.broadcasted_iota(jnp.int32, (grp, n_cols), 0) & (ts - 1)) + N_META + past
    valid = ecol < N_META + ts
    chunk = lambda p: jnp.where(p < N_META, -1, (p - N_META) >> CHUNK_SHIFT)
    mask_chunk = jnp.logical_and(valid, chunk(pos_k) <= chunk(pos_q))
    mask_strict = jnp.logical_and(valid, pos_k < pos_q)

    lo, hi = _lane_half_masks((ts, LANES))
    zero = jnp.zeros((ts, LANES), _bf16)
    blocks = [slice(j * LANES, (j + 1) * LANES) for j in range(DIFF_WIDTH // LANES)]

    def stacked(q):
        return jnp.concatenate([jnp.where(lo, q, zero), jnp.where(hi, q, zero)], axis=0)

    def scores(q_ref, cache_ref, edge):
        out = []
        for sl in blocks:
            q = stacked(q_ref[:, sl])
            out.append(jnp.concatenate([_dot(q, cache_ref[0, sl, :].astype(_bf16)),
                                        _dot_nt(q, edge[:, sl])], axis=1))
        return out

    diff_scores = scores(qd_ref, ckd_ref, kd_edge)
    sb_scores = scores(qs_ref, cks_ref, ks_edge)
    yield

    lam = _lambda(lq1_ref[...], lk1_ref[...], lq2_ref[...], lk2_ref[...])
    probs, sums = [], []
    for s in diff_scores:
        s = jnp.where(mask_chunk, s, NEG_INF)
        p = jnp.exp2(s - jnp.max(s, axis=-1, keepdims=True))
        probs.append(p.astype(_bf16))
        sums.append(jnp.sum(p, axis=-1, keepdims=True))
    for h, (sl, p, l) in enumerate(zip(blocks, probs, sums)):
        v_cache = cvd_ref[0, pl.ds(h, past, stride=N_HEADS_DIFF), :].astype(_bf16)
        pv = _dot(p[:, :past], v_cache) + _dot(p[:, past:], vd_edge[:, sl])
        out = _diff_finish(pv[:ts], l[:ts], pv[ts:], l[ts:], lam, gh_ref[...])
        od_ref[:, sl] = out.astype(_bf16)
    yield

    n_tiles = past // LANES
    tiles = [slice(t * LANES, (t + 1) * LANES) for t in range(n_tiles)]
    edge_cols = slice(past, n_cols)
    log_betas, log_keeps, cache_pieces, edge_pieces = [], [], [], []
    for s in sb_scores:
        log_beta = _log2_sigmoid(s)
        log_keep = jnp.where(mask_strict, log_beta - s, 0.0)
        keep_hi = log_keep.astype(_bf16)
        keep_lo = (log_keep - keep_hi.astype(_f32)).astype(_bf16)
        log_betas.append(log_beta)
        log_keeps.append(log_keep)
        cache_pieces += [keep_hi[:, t] for t in tiles] + [keep_lo[:, t] for t in tiles]
        edge_pieces += [keep_hi[:, edge_cols], keep_lo[:, edge_cols]]
    after_cache = _dot(jnp.concatenate(cache_pieces, axis=0), _suffix_matrix(LANES))
    ej = lax.broadcasted_iota(jnp.int32, (LANES, LANES), 0)
    es = lax.broadcasted_iota(jnp.int32, (LANES, LANES), 1)
    edge_suffix = jnp.where(jnp.logical_and(ej > es, (ej < N_META) == (es < N_META)), 1.0, 0.0)
    after_edge = _dot(jnp.concatenate(edge_pieces, axis=0), edge_suffix.astype(_bf16))
    yield
    is_meta = lax.broadcasted_iota(jnp.int32, (grp, LANES), 1) < N_META
    for j, sl in enumerate(blocks):
        base = j * 2 * n_tiles
        after = [after_cache[(base + t) * grp:(base + t + 1) * grp]
                 + after_cache[(base + n_tiles + t) * grp:(base + n_tiles + t + 1) * grp]
                 for t in range(n_tiles)]
        edge_after = (after_edge[2 * j * grp:(2 * j + 1) * grp]
                      + after_edge[(2 * j + 1) * grp:(2 * j + 2) * grp])
        keep_edge = log_keeps[j][:, edge_cols]
        carry = edge_after[:, N_META:N_META + 1] + keep_edge[:, N_META:N_META + 1]
        shifted = [None] * n_tiles
        for t in reversed(range(n_tiles)):
            shifted[t] = after[t] + carry
            carry = carry + after[t][:, 0:1] + log_keeps[j][:, tiles[t]][:, 0:1]
        shifted.append(edge_after + jnp.where(is_meta, carry, 0.0))
        a = jnp.exp2(log_betas[j] + jnp.concatenate(shifted, axis=1))
        a = jnp.where(mask_strict, a, 0.0).astype(_bf16)
        pv = _dot_nt(a[:, :past], cvs_ref[0, sl, :].astype(_bf16)) + _dot(a[:, past:], vs_edge[:, sl])
        os_ref[:, sl] = jnp.where(lo, pv[:ts], pv[ts:]).astype(_bf16)


N_SAMP_INPUTS = 19


def _samp_call_parts(args, ts):
    assert len(args) == N_SAMP_INPUTS
    nb, _, past = args[10].shape
    assert past % LANES == 0 and N_META + ts <= LANES
    row_blk = pl.BlockSpec((ts, DIFF_WIDTH), lambda b: (b, 0))
    per_batch = lambda a: pl.BlockSpec((1,) + a.shape[1:], lambda b: (b, 0, 0))
    small = lambda a: pl.BlockSpec(a.shape, lambda b: (0, 0))
    out = jax.ShapeDtypeStruct((nb * ts, DIFF_WIDTH), _bf16)
    in_specs = ([row_blk] * 6 + [small(a) for a in args[6:10]] + [per_batch(a) for a in args[10:14]]
                + [small(a) for a in args[14:]])
    scratch = [pltpu.VMEM((LANES, DIFF_WIDTH), _bf16)] * 4
    return dict(ts=ts, past=past), nb, in_specs, [row_blk, row_blk], [out, out], scratch


def _samp_attn(args, ts):
    kwargs, nb, in_specs, out_specs, out_shape, scratch = _samp_call_parts(args, ts)
    return pl.pallas_call(
        functools.partial(_samp_kernel, **kwargs),
        grid=(nb,),
        in_specs=in_specs, out_specs=out_specs, out_shape=out_shape, scratch_shapes=scratch,
        compiler_params=pltpu.CompilerParams(
            dimension_semantics=("arbitrary",), vmem_limit_bytes=VMEM_LIMIT),
        name="samp_attn",
    )(*args)


def _mlp_kernel(*refs, tf):
    _run_phases(_mlp_phases(*refs, tf=tf))


def _mlp_phases(x_ref, md_ref, ms_ref, wo_ref, gm_ref, wu_ref, wd_ref, gf_ref, o_ref, h2_ref, *, tf):
    x1 = (x_ref[...] + _dot(md_ref[...], wo_ref[:DIFF_WIDTH, :])
          + _dot(ms_ref[...], wo_ref[DIFF_WIDTH:, :]))
    h2_ref[...] = _rms(x1, gm_ref[...]).astype(_bf16)
    o_ref[...] = x1
    yield
    for f in range(D_FF // tf):
        u = jnp.maximum(_dot(h2_ref[...], wu_ref[:, f * tf:(f + 1) * tf]), 0.0)
        o_ref[...] += _dot((u * u).astype(_bf16), wd_ref[f * tf:(f + 1) * tf, :])
        yield
    o_ref[...] = _rms(o_ref[...], gf_ref[...])


N_MLP_INPUTS = 8
MLP_ROWS = 512
MLP_FF_CHUNK = 1024


def _mlp_call_parts(args):
    assert len(args) == N_MLP_INPUTS
    x, _, _, wo, g_mlp, wu, wd, g_final = args
    rows, tm = x.shape[0], MLP_ROWS
    resident = lambda a: pl.BlockSpec(a.shape, lambda i: (0, 0), pipeline_mode=pl.Buffered(1))
    small = lambda a: pl.BlockSpec(a.shape, lambda i: (0, 0))
    row_blk = lambda width: pl.BlockSpec((tm, width), lambda i: (i, 0))
    in_specs = [row_blk(D_MODEL), row_blk(DIFF_WIDTH), row_blk(SB_WIDTH),
                resident(wo), small(g_mlp), resident(wu), resident(wd), small(g_final)]
    return (rows // tm, in_specs, row_blk(D_MODEL), jax.ShapeDtypeStruct((rows, D_MODEL), _f32),
            [pltpu.VMEM((tm, D_MODEL), _bf16)])


def _mlp(*args):
    steps, in_specs, out_spec, out_shape, scratch = _mlp_call_parts(args)
    return pl.pallas_call(
        functools.partial(_mlp_kernel, tf=MLP_FF_CHUNK),
        grid=(steps,),
        in_specs=in_specs, out_specs=out_spec, out_shape=out_shape, scratch_shapes=scratch,
        compiler_params=pltpu.CompilerParams(
            dimension_semantics=("arbitrary",), vmem_limit_bytes=VMEM_LIMIT),
        name="mlp",
    )(*args)


def _mlp_samp_kernel(*refs, samp_kwargs):
    n_in = N_MLP_INPUTS + N_SAMP_INPUTS
    mlp_in, samp_in = refs[:N_MLP_INPUTS], refs[N_MLP_INPUTS:n_in]
    y_ref, od_ref, os_ref, h2_ref = refs[n_in:n_in + 4]
    edges = refs[n_in + 4:]
    _run_phases(_samp_phases(*samp_in, od_ref, os_ref, *edges, **samp_kwargs),
                _mlp_phases(*mlp_in, y_ref, h2_ref, tf=MLP_FF_CHUNK))


def _mlp_and_samp_attn(mlp_args, samp_args, ts):
    steps, m_in, m_out, m_shape, m_scratch = _mlp_call_parts(mlp_args)
    kwargs, nb, s_in, s_out, s_shape, s_scratch = _samp_call_parts(samp_args, ts)
    if steps != nb:
        return (_mlp(*mlp_args),) + tuple(_samp_attn(samp_args, ts))
    return pl.pallas_call(
        functools.partial(_mlp_samp_kernel, samp_kwargs=kwargs),
        grid=(steps,),
        in_specs=m_in + s_in, out_specs=[m_out] + s_out, out_shape=[m_shape] + s_shape,
        scratch_shapes=m_scratch + s_scratch,
        compiler_params=pltpu.CompilerParams(
            dimension_semantics=("arbitrary",), vmem_limit_bytes=VMEM_LIMIT),
        name="mlp_samp_attn",
    )(*mlp_args, *samp_args)


def kernel(x_prompt, x_sample, cache_diff_k, cache_diff_v, cache_sb_k, cache_sb_v, meta_tokens, g_mix, w_in, lambda_q1, lambda_k1, lambda_q2, lambda_k2, g_diff_head, w_out, g_mlp, w_up, w_down, g_final):
    bp, tp, d = x_prompt.shape
    bs, ts, _ = x_sample.shape
    past = cache_diff_k.shape[2]
    assert bp == 1 and g_mix.shape[0] == 1, "single prompt sequence, depth 1"

    w_in_b = w_in[0].astype(_bf16)
    w_out_b = w_out[0].astype(_bf16)
    w_up_b = w_up[0].astype(_bf16)
    w_down_b = w_down[0].astype(_bf16)
    g_final2 = g_final[None, :]
    lams = (lambda_q1, lambda_k1, lambda_q2, lambda_k2)

    x_ms = jnp.concatenate([meta_tokens.astype(x_sample.dtype), x_sample.reshape(bs * ts, d)], axis=0)
    pos_ms = jnp.concatenate([jnp.arange(N_META), jnp.tile(N_META + past + jnp.arange(ts), bs)])
    cos_s, sin_s = _rope_tables(pos_ms)
    s_out = _proj(x_ms, g_mix, w_in_b, cos_s, sin_s)
    meta_f32 = [a[:N_META] for a in s_out[2:6]]
    samp_f32 = [a[N_META:] for a in s_out[2:6]]
    pad_meta = lambda a: jnp.pad(a[:N_META], ((0, META_PAD - N_META), (0, 0)))
    kdm, ksm, vdm, vsm = [pad_meta(a) for a in s_out[6:10]]
    qd_s, qs_s = s_out[0][N_META:], s_out[1][N_META:]
    kdn, ksn, vdn, vsn = [a[N_META:] for a in s_out[6:10]]

    p_out = _proj_prompt(x_prompt[0], g_mix, w_in_b, N_META, meta_f32, tm=512)
    qd, qs, kd, ks, vd, vs, kdb, ksb, vdb, vsb = p_out

    mixed_d = _diff_attn(qd, kdb, vdb, kdm, vdm, *lams, g_diff_head)
    mixed_s = _sb_attn(qs, ksb, vsb, ksm, vsm)

    feat_major = lambda c: jnp.transpose(c[0].reshape(bs, past, DIFF_WIDTH), (0, 2, 1))
    mlp_weights = (w_out_b, g_mlp, w_up_b, w_down_b, g_final2)
    samp_args = (qd_s, qs_s, kdn, ksn, vdn, vsn, kdm, ksm, vdm, vsm,
                 feat_major(cache_diff_k), feat_major(cache_sb_k),
                 cache_diff_v[0].reshape(bs, past * N_HEADS_DIFF, 2 * HEAD_DIM),
                 feat_major(cache_sb_v), *lams, g_diff_head)
    y_prompt, smix_d, smix_s = _mlp_and_samp_attn((x_prompt[0], mixed_d, mixed_s) + mlp_weights,
                                                  samp_args, ts)
    y_sample = _mlp(x_sample.reshape(bs * ts, d), smix_d, smix_s, *mlp_weights)

    def prompt_cache(a, tail):
        return a.reshape((1, bp, N_META + tp) + tail)

    def sample_cache(a, tail):
        return a.reshape((1, bs, ts) + tail)

    dk_tail, dv_tail = (N_HEADS_DIFF, 2, HEAD_DIM), (N_HEADS_DIFF, 2 * HEAD_DIM)
    sb_tail = (N_HEADS_SB, HEAD_DIM)
    return (y_prompt[None], y_sample.reshape(bs, ts, d),
            prompt_cache(kd, dk_tail), prompt_cache(vd, dv_tail),
            prompt_cache(ks, sb_tail), prompt_cache(vs, sb_tail),
            sample_cache(samp_f32[0], dk_tail), sample_cache(samp_f32[2], dv_tail),
            sample_cache(samp_f32[1], sb_tail), sample_cache(samp_f32[3], sb_tail))
```

```python
import functools
import math

import jax
import jax.numpy as jnp
from jax import lax
from jax.experimental import pallas as pl
from jax.experimental.pallas import tpu as pltpu

D_MODEL = 1024
CHUNK = 64
CHUNK_SHIFT = 6
N_META = 16
HEAD_DIM = 64
N_HEADS_DIFF = 4
N_HEADS_SB = 8
DIFF_WIDTH = N_HEADS_DIFF * 2 * HEAD_DIM
SB_WIDTH = N_HEADS_SB * HEAD_DIM
MIX_WIDTH = DIFF_WIDTH + SB_WIDTH
D_IN = 3 * MIX_WIDTH
D_FF = 4 * D_MODEL
ROPE_THETA = 10000.0
NORM_EPS = 1e-6
NEG_INF = -1e30
LAMBDA_INIT = 0.8 - 0.6 * math.exp(-0.3 * 0)
LOG2_E = math.log2(math.e)

LANES = 128
META_PAD = 128
SB_LOG2_STOP = -44.0
SCORE_LEAD = 3

VMEM_LIMIT = 56 * 1024 * 1024

_f32 = jnp.float32
_bf16 = jnp.bfloat16


def _dot(a, b):
    return jnp.dot(a, b, preferred_element_type=_f32)


def _dot_nt(a, b):
    return lax.dot_general(a, b, (((1,), (1,)), ((), ())), preferred_element_type=_f32)


def _rms(x, g):
    return x * lax.rsqrt(jnp.mean(x * x, axis=-1, keepdims=True) + NORM_EPS) * g


def _log2_sigmoid(s2):
    return jnp.minimum(s2, 0.0) - jnp.log2(1.0 + jnp.exp2(-jnp.abs(s2)))


def _first_half_lanes(shape):
    lane = lax.broadcasted_iota(jnp.int32, shape, 1)
    return (lane & (HEAD_DIM - 1)) < (HEAD_DIM // 2)


def _proj_math(x_ref, g_ref, w_ref, cos, sin):
    h = _rms(x_ref[...], g_ref[...]).astype(_bf16)
    first_half = _first_half_lanes(cos.shape)

    def rope(z):
        parts = []
        for j in range(z.shape[1] // LANES):
            zj = z[:, j * LANES:(j + 1) * LANES]
            swapped = jnp.where(first_half,
                                pltpu.roll(zj, LANES - HEAD_DIM // 2, 1),
                                pltpu.roll(zj, HEAD_DIM // 2, 1))
            parts.append(zj * cos + swapped * sin)
        return jnp.concatenate(parts, axis=1)

    def cols(c):
        return _dot(h, w_ref[:, c * DIFF_WIDTH:(c + 1) * DIFF_WIDTH])

    scale = HEAD_DIM ** -0.5
    scale = scale * LOG2_E
    return (rope(cols(0)) * scale, cols(1) * scale, rope(cols(2)), cols(3), cols(4), cols(5))


def _proj_kernel(x_ref, g_ref, w_ref, cos_ref, sin_ref, qd_ref, qs_ref, *kv_refs):
    qd, qs, kd, ks, vd, vs = _proj_math(x_ref, g_ref, w_ref, cos_ref[...], sin_ref[...])
    qd_ref[...] = qd[N_META:].astype(_bf16)
    qs_ref[...] = qs[N_META:].astype(_bf16)
    n = len(kv_refs) // 4
    meta_f32, samp_f32, meta_b16, samp_b16 = (kv_refs[i * n:(i + 1) * n] for i in range(4))
    for j, val in enumerate((kd, ks, vd, vs)):
        meta_f32[j][...] = val[:N_META]
        samp_f32[j][...] = val[N_META:]
        b16 = val.astype(_bf16)
        meta_b16[j][:N_META, :] = b16[:N_META]
        meta_b16[j][N_META:, :] = jnp.zeros((META_PAD - N_META, DIFF_WIDTH), _bf16)
        samp_b16[j][...] = b16[N_META:]


def _proj(x, g, w_bf16, cos, sin):
    rows = x.shape[0] - N_META
    full = lambda a: pl.BlockSpec(a.shape, lambda i: (0, 0))
    out = lambda r, dtype: jax.ShapeDtypeStruct((r, DIFF_WIDTH), dtype)
    out_shape = ([out(rows, _bf16)] * 2 + [out(N_META, _f32)] * 4 + [out(rows, _f32)] * 4
                 + [out(META_PAD, _bf16)] * 4 + [out(rows, _bf16)] * 4)
    res = pl.pallas_call(
        _proj_kernel,
        grid=(1,),
        in_specs=[full(x), full(g), full(w_bf16), full(cos), full(sin)],
        out_specs=[full(o) for o in out_shape],
        out_shape=out_shape,
        compiler_params=pltpu.CompilerParams(
            dimension_semantics=("arbitrary",), vmem_limit_bytes=VMEM_LIMIT),
        name="proj",
    )(x, g, w_bf16, cos, sin)
    return res[0], res[1], res[2:6], res[6:10], res[10:14], res[14:18]


def _proj_prompt_kernel(x_ref, g_ref, w_ref, cos_row_ref, sin_row_ref, cos_blk_ref, sin_blk_ref,
                        mkd_ref, mks_ref, mvd_ref, mvs_ref,
                        qd_ref, qs_ref, kd_ref, ks_ref, vd_ref, vs_ref,
                        kdb_ref, ksb_ref, vdb_ref, vsb_ref, tail_ref, *, tm):
    i = pl.program_id(0)
    last = pl.num_programs(0) - 1
    row_outs = ((0, kd_ref), (1, ks_ref), (3, vs_ref))

    def put_vd(token0, rows):
        for h in range(N_HEADS_DIFF):
            vd_ref[pl.ds(token0 * N_HEADS_DIFF + h, rows.shape[0], stride=N_HEADS_DIFF), :] = (
                rows[:, h * LANES:(h + 1) * LANES])

    @pl.when(i == 0)
    def _():
        for j, meta in enumerate((mkd_ref, mks_ref, mvd_ref, mvs_ref)):
            tail_ref[j] = meta[...]

    @pl.when(i < last)
    def _():
        cos_r, sin_r, cos_b, sin_b = cos_row_ref[...], sin_row_ref[...], cos_blk_ref[0], sin_blk_ref[0]
        cos = cos_b * cos_r - sin_b * sin_r
        sin = sin_b * cos_r + cos_b * sin_r
        sin = jnp.where(_first_half_lanes(sin.shape), -sin, sin)
        qd, qs, kd, ks, vd, vs = _proj_math(x_ref, g_ref, w_ref, cos, sin)
        qd_ref[...] = qd.astype(_bf16)
        qs_ref[...] = qs.astype(_bf16)
        for b16_ref, val in ((kdb_ref, kd), (ksb_ref, ks), (vdb_ref, vd), (vsb_ref, vs)):
            b16_ref[...] = val.astype(_bf16)
        for (j, out), val in zip(row_outs, (kd, ks, vs)):
            out[:N_META, :] = tail_ref[j]
            out[N_META:, :] = val[:tm - N_META]
            tail_ref[j] = val[tm - N_META:]
        put_vd(0, tail_ref[2])
        put_vd(N_META, vd[:tm - N_META])
        tail_ref[2] = vd[tm - N_META:]

    @pl.when(i == last)
    def _():
        for j, out in row_outs:
            out[:N_META, :] = tail_ref[j]
        put_vd(0, tail_ref[2])


def _proj_prompt(x, g, w_bf16, pos0, meta_kv, tm):
    rows = x.shape[0]
    nblk = rows // tm
    cos_row, sin_row = _angle_tables(jnp.arange(tm))
    cos_blk, sin_blk = [t[:, None, :] for t in _angle_tables(pos0 + tm * jnp.arange(nblk))]
    frame_blk = lambda width: pl.BlockSpec((tm, width), lambda i: (jnp.minimum(i, nblk - 1), 0))
    blk_angle = pl.BlockSpec((1, 1, LANES), lambda i: (jnp.minimum(i, nblk - 1), 0, 0))
    full = lambda a: pl.BlockSpec(a.shape, lambda i: (0, 0))
    f32_out = jax.ShapeDtypeStruct((N_META + rows, DIFF_WIDTH), _f32)
    vd_out = jax.ShapeDtypeStruct(((N_META + rows) * N_HEADS_DIFF, LANES), _f32)
    b16_out = jax.ShapeDtypeStruct((rows, DIFF_WIDTH), _bf16)
    stream_blk = pl.BlockSpec((tm, DIFF_WIDTH), lambda i: (i, 0))
    return pl.pallas_call(
        functools.partial(_proj_prompt_kernel, tm=tm),
        grid=(nblk + 1,),
        in_specs=[frame_blk(D_MODEL), full(g), full(w_bf16), full(cos_row), full(sin_row),
                  blk_angle, blk_angle] + [full(m) for m in meta_kv],
        out_specs=[frame_blk(DIFF_WIDTH)] * 2
                  + [stream_blk, stream_blk,
                     pl.BlockSpec((tm * N_HEADS_DIFF, LANES), lambda i: (i, 0)), stream_blk]
                  + [frame_blk(DIFF_WIDTH)] * 4,
        out_shape=[b16_out, b16_out, f32_out, f32_out, vd_out, f32_out,
                   b16_out, b16_out, b16_out, b16_out],
        scratch_shapes=[pltpu.VMEM((4, N_META, DIFF_WIDTH), _f32)],
        compiler_params=pltpu.CompilerParams(
            dimension_semantics=("arbitrary",), vmem_limit_bytes=VMEM_LIMIT),
        name="proj_prompt",
    )(x, g, w_bf16, cos_row, sin_row, cos_blk, sin_blk, *meta_kv)


def _angle_tables(pos):
    half = HEAD_DIM // 2
    inv_freq = ROPE_THETA ** (-jnp.arange(half, dtype=_f32) / half)
    ang = pos.astype(_f32)[:, None] * inv_freq[None, :]
    reps = LANES // half
    return jnp.tile(jnp.cos(ang), (1, reps)), jnp.tile(jnp.sin(ang), (1, reps))


def _rope_tables(pos):
    cos, sin = _angle_tables(pos)
    return cos, jnp.where(_first_half_lanes(sin.shape), -sin, sin)


def _lane_half_masks(shape):
    lane = lax.broadcasted_iota(jnp.int32, shape, 1)
    return lane < HEAD_DIM, lane >= HEAD_DIM


def _lambda(lq1, lk1, lq2, lk2):
    return (jnp.exp(jnp.sum(lq1 * lk1, axis=-1, keepdims=True))
            - jnp.exp(jnp.sum(lq2 * lk2, axis=-1, keepdims=True)) + LAMBDA_INIT)


def _diff_finish(o0, l0, o1, l1, lam, g_head):
    o = o0 / l0 - lam * (o1 / l1)
    return _rms(o, g_head) * (1.0 - LAMBDA_INIT)


def _diff_kernel(q_ref, k_ref, v_ref, km_ref, vm_ref, lq1_ref, lk1_ref, lq2_ref, lk2_ref, gh_ref,
                 o_ref, m_ref, acc_ref, *, bq, bk, heads):
    qi = pl.program_id(1)
    lo, hi = _lane_half_masks((bq, LANES))
    zero = jnp.zeros((bq, LANES), _bf16)
    chains = []
    for a in range(heads):
        sl = slice(a * LANES, (a + 1) * LANES)
        q = q_ref[:, sl]
        chains.append((2 * a, sl, jnp.where(lo, q, zero)))
        chains.append((2 * a + 1, sl, jnp.where(hi, q, zero)))

    def softmax_pv(c, s, v, mask, first):
        if mask is not None:
            s = jnp.where(mask, s, NEG_INF)
        m_cur = jnp.max(s, axis=-1, keepdims=True)
        m_new = jnp.broadcast_to(m_cur, (bq, LANES)) if first else jnp.maximum(m_ref[c], m_cur)
        p = jnp.exp2((s - jnp.tile(m_new, (1, s.shape[1] // LANES))).astype(_bf16))
        pv = _dot(p, jnp.concatenate([v, jnp.ones_like(v)], axis=1))
        if first:
            acc_ref[c] = pv
        else:
            alpha = jnp.exp2(m_ref[c] - m_new)
            acc_ref[c] = jnp.tile(alpha, (1, 2)) * acc_ref[c] + pv
        m_ref[c] = m_new

    def step(k_of, v_of, mask, first):
        scores = {}
        for i in range(len(chains) + SCORE_LEAD):
            if i < len(chains):
                _, sl, qz = chains[i]
                scores[i] = _dot_nt(qz, k_of(sl))
            if i >= SCORE_LEAD:
                c, sl, _ = chains[i - SCORE_LEAD]
                softmax_pv(c, scores.pop(i - SCORE_LEAD), v_of(sl), mask, first)

    q0 = qi * bq
    n_full = (q0 + CHUNK) // bk

    start = pl.multiple_of(n_full * bk, bk)
    row = lax.broadcasted_iota(jnp.int32, (bq, bk + META_PAD), 0) + q0
    col = lax.broadcasted_iota(jnp.int32, (bq, bk + META_PAD), 1)
    key_chunk = jnp.where(col < bk, (col + start) >> CHUNK_SHIFT,
                          jnp.where(col < bk + N_META, -1, jnp.iinfo(jnp.int32).max))
    mask = key_chunk <= (row >> CHUNK_SHIFT)
    step(lambda sl: jnp.concatenate([k_ref[pl.ds(start, bk), sl], km_ref[:, sl]], axis=0),
         lambda sl: jnp.concatenate([v_ref[pl.ds(start, bk), sl], vm_ref[:, sl]], axis=0), mask, True)

    def full_keys(start, width):
        start = pl.multiple_of(start, bk)
        step(lambda sl: k_ref[pl.ds(start, width), sl], lambda sl: v_ref[pl.ds(start, width), sl],
             None, False)

    def tile_pair(j, carry):
        full_keys(j * 2 * bk, 2 * bk)
        return carry

    lax.fori_loop(0, n_full // 2, tile_pair, 0)

    @pl.when(n_full % 2 == 1)
    def _():
        full_keys((n_full - 1) * bk, bk)

    lam = _lambda(lq1_ref[...], lk1_ref[...], lq2_ref[...], lk2_ref[...])
    for a in range(heads):
        acc0, acc1 = acc_ref[2 * a], acc_ref[2 * a + 1]
        out = _diff_finish(acc0[:, :LANES], acc0[:, LANES:], acc1[:, :LANES], acc1[:, LANES:],
                           lam, gh_ref[...])
        o_ref[:, a * LANES:(a + 1) * LANES] = out.astype(_bf16)


def _diff_attn(qd, kdb, vdb, kmeta, vmeta, lq1, lk1, lq2, lk2, g_head, bq=512, bk=512,
               heads=N_HEADS_DIFF):
    n = qd.shape[0]
    width = heads * LANES
    head_blk = lambda rows: pl.BlockSpec((rows, width), lambda h, i: (0, h),
                                         pipeline_mode=pl.Buffered(1))
    small = lambda a: pl.BlockSpec(a.shape, lambda h, i: (0, 0))
    return pl.pallas_call(
        functools.partial(_diff_kernel, bq=bq, bk=bk, heads=heads),
        grid=(N_HEADS_DIFF // heads, n // bq),
        in_specs=[pl.BlockSpec((bq, width), lambda h, i: (i, h)),
                  head_blk(n), head_blk(n), head_blk(META_PAD), head_blk(META_PAD),
                  small(lq1), small(lk1), small(lq2), small(lk2), small(g_head)],
        out_specs=pl.BlockSpec((bq, width), lambda h, i: (i, h)),
        out_shape=jax.ShapeDtypeStruct((n, DIFF_WIDTH), _bf16),
        scratch_shapes=[pltpu.VMEM((2 * heads, bq, LANES), _f32),
                        pltpu.VMEM((2 * heads, bq, 2 * LANES), _f32)],
        compiler_params=pltpu.CompilerParams(
            dimension_semantics=("arbitrary", "arbitrary"), vmem_limit_bytes=VMEM_LIMIT),
        name="diff_attn",
    )(qd, kdb, vdb, kmeta, vmeta, lq1, lk1, lq2, lk2, g_head)


def _suffix_matrix(n):
    j = lax.broadcasted_iota(jnp.int32, (n, n), 0)
    s = lax.broadcasted_iota(jnp.int32, (n, n), 1)
    return jnp.where(j > s, 1.0, 0.0).astype(_bf16)


def _sb_tile_phases(queries, keys, values, carries, mask, suffix, out):
    masks = mask if isinstance(mask, (list, tuple)) else [mask] * len(queries)
    scores = [_dot_nt(q, k) for q, k in zip(queries, keys)]
    yield
    log_betas, log_keeps = [], []
    for s, m in zip(scores, masks):
        log_beta = _log2_sigmoid(s)
        log_keep = log_beta - s
        if m is not None:
            log_keep = jnp.where(m, log_keep, 0.0)
        log_betas.append(log_beta)
        log_keeps.append(log_keep)
    afters = [_dot(log_keep.astype(_bf16), suffix) for log_keep in log_keeps]
    yield
    weights, new_carries = [], []
    for log_beta, log_keep, after, carry, m in zip(log_betas, log_keeps, afters, carries, masks):
        a = jnp.exp2(log_beta + after + carry)
        if m is not None:
            a = jnp.where(m, a, 0.0)
        weights.append(a.astype(_bf16))
        new_carries.append(carry + after[:, 0:1] + log_keep[:, 0:1])
    out.append(([_dot(a, v) for a, v in zip(weights, values)], new_carries))


def _sb_kernel(q_ref, k_ref, v_ref, km_ref, vm_ref, o_ref, acc_ref, carry_ref, live_ref, *, bq):
    qi = pl.program_id(0)
    lo, hi = _lane_half_masks((bq, LANES))
    zero = jnp.zeros((bq, LANES), _bf16)
    pairs = [slice(p * LANES, (p + 1) * LANES) for p in range(SB_WIDTH // LANES)]
    lanes = [sl for sl in pairs for _ in range(2)]
    heads = range(N_HEADS_SB)
    bk = 2 * bq
    queries = []
    for sl in pairs:
        q = q_ref[:, sl]
        queries += [jnp.where(lo, q, zero), jnp.where(hi, q, zero)]

    def older_tile(keys, values, mask, width):
        result = []
        _run_phases(_sb_tile_phases(queries, keys, values, [carry_ref[c] for c in heads], mask,
                                    _suffix_matrix(width), result))
        pvs, carries = result[0]
        live = carries[0]
        for c in heads:
            acc_ref[c] += pvs[c]
            carry_ref[c] = carries[c]
            live = jnp.maximum(live, carries[c])
        return jnp.max(live)

    def frame_tile(start, width):
        start = pl.multiple_of(start, bq)
        return older_tile([k_ref[pl.ds(start, width), sl] for sl in lanes],
                          [v_ref[pl.ds(start, width), sl] for sl in lanes], None, width)

    def half_tiles(starts, masks, first):
        qs2, ks2, vs2, carries = [], [], [], []
        for h, rows in enumerate((slice(0, half), slice(half, bq))):
            start = pl.multiple_of(starts[h], half)
            for c in heads:
                qs2.append(queries[c][rows])
                ks2.append(k_ref[pl.ds(start, bq), lanes[c]])
                vs2.append(v_ref[pl.ds(start, bq), lanes[c]])
                carries.append(jnp.zeros((half, 1), _f32) if first else carry_ref[c, rows])
        result = []
        _run_phases(_sb_tile_phases(qs2, ks2, vs2, carries, [m for m in masks for _ in heads],
                                    _suffix_matrix(bq), result))
        pvs, carries = result[0]
        live = carries[0]
        for h, rows in enumerate((slice(0, half), slice(half, bq))):
            for c in heads:
                i = h * N_HEADS_SB + c
                acc_ref[c, rows] = pvs[i] if first else acc_ref[c, rows] + pvs[i]
                carry_ref[c, rows] = carries[i]
                live = jnp.maximum(live, carries[i])
        return jnp.max(live)

    q0 = qi * bq
    half = bq // 2
    start_a = jnp.maximum(q0 - half, 0)
    row = lax.broadcasted_iota(jnp.int32, (half, bq), 0) + q0
    col = lax.broadcasted_iota(jnp.int32, (half, bq), 1)
    live_ref[0] = half_tiles((start_a, q0), (col + start_a < row, col + q0 < row + half), True)

    @pl.when(jnp.logical_and(q0 > 0, live_ref[0] > SB_LOG2_STOP))
    def _():
        live_ref[0] = half_tiles((q0 - bq, q0 - bq), (col < half, None), False)

    start0 = jnp.maximum(q0 - bq, 0)

    def cond(state):
        start, live = state
        return jnp.logical_and(start >= bk, live > SB_LOG2_STOP)

    def body(state):
        start, _ = state
        return start - bk, frame_tile(start - bk, bk)

    start, live = lax.while_loop(cond, body, (start0, live_ref[0]))
    live_ref[0] = live

    @pl.when(jnp.logical_and(start == bq, live > SB_LOG2_STOP))
    def _():
        live_ref[0] = frame_tile(0, bq)

    @pl.when(jnp.logical_and(start < bk, live_ref[0] > SB_LOG2_STOP))
    def _():
        mcol = lax.broadcasted_iota(jnp.int32, (bq, META_PAD), 1)
        older_tile([km_ref[:, sl] for sl in lanes], [vm_ref[:, sl] for sl in lanes], mcol < N_META,
                   META_PAD)

    for p, sl in enumerate(pairs):
        o_ref[:, sl] = jnp.where(lo, acc_ref[2 * p], acc_ref[2 * p + 1]).astype(_bf16)


def _sb_attn(qs, ksb, vsb, kmeta, vmeta, bq=128):
    n = qs.shape[0]
    resident = lambda a: pl.BlockSpec(a.shape, lambda i: (0, 0), pipeline_mode=pl.Buffered(1))
    return pl.pallas_call(
        functools.partial(_sb_kernel, bq=bq),
        grid=(n // bq,),
        in_specs=[pl.BlockSpec((bq, SB_WIDTH), lambda i: (i, 0)),
                  resident(ksb), resident(vsb), resident(kmeta), resident(vmeta)],
        out_specs=pl.BlockSpec((bq, SB_WIDTH), lambda i: (i, 0)),
        out_shape=jax.ShapeDtypeStruct((n, SB_WIDTH), _bf16),
        scratch_shapes=[pltpu.VMEM((N_HEADS_SB, bq, LANES), _f32),
                        pltpu.VMEM((N_HEADS_SB, bq, 1), _f32),
                        pltpu.SMEM((1,), _f32)],
        compiler_params=pltpu.CompilerParams(
            dimension_semantics=("arbitrary",), vmem_limit_bytes=VMEM_LIMIT),
        name="sb_attn",
    )(qs, ksb, vsb, kmeta, vmeta)


def _run_phases(*phase_generators):
    pending = list(phase_generators)
    while pending:
        pending = [g for g in pending if next(g, "done") != "done"]


def _samp_kernel(*refs, ts, past):
    _run_phases(_samp_phases(*refs, ts=ts, past=past))


def _samp_phases(qd_ref, qs_ref, kdn_ref, ksn_ref, vdn_ref, vsn_ref,
                 kdm_ref, ksm_ref, vdm_ref, vsm_ref,
                 ckd_ref, cks_ref, cvd_ref, cvs_ref,
                 lq1_ref, lk1_ref, lq2_ref, lk2_ref, gh_ref,
                 od_ref, os_ref, kd_edge, ks_edge, vd_edge, vs_edge, *, ts, past):
    for edge, meta, new in ((kd_edge, kdm_ref, kdn_ref), (ks_edge, ksm_ref, ksn_ref),
                            (vd_edge, vdm_ref, vdn_ref), (vs_edge, vsm_ref, vsn_ref)):
        edge[0:N_META, :] = meta[0:N_META, :]
        edge[N_META:N_META + ts, :] = new[...]
        edge[N_META + ts:, :] = jnp.zeros((LANES - N_META - ts, DIFF_WIDTH), _bf16)

    grp = 2 * ts
    n_cols = past + LANES
    col = lax.broadcasted_iota(jnp.int32, (grp, n_cols), 1)
    ecol = col - past
    pos_k = jnp.where(ecol < 0, col + N_META, jnp.where(ecol < N_META, ecol, ecol + past))
    pos_q = (lax.broadcasted_iota(jnp.int32, (grp, n_cols), 0) & (ts - 1)) + N_META + past
    valid = ecol < N_META + ts
    chunk = lambda p: jnp.where(p < N_META, -1, (p - N_META) >> CHUNK_SHIFT)
    mask_chunk = jnp.logical_and(valid, chunk(pos_k) <= chunk(pos_q))
    mask_strict = jnp.logical_and(valid, pos_k < pos_q)

    lo, hi = _lane_half_masks((ts, LANES))
    zero = jnp.zeros((ts, LANES), _bf16)
    blocks = [slice(j * LANES, (j + 1) * LANES) for j in range(DIFF_WIDTH // LANES)]

    def stacked(q):
        return jnp.concatenate([jnp.where(lo, q, zero), jnp.where(hi, q, zero)], axis=0)

    def scores(q_ref, cache_ref, edge):
        out = []
        for sl in blocks:
            q = stacked(q_ref[:, sl])
            out.append(jnp.concatenate([_dot(q, cache_ref[0, sl, :].astype(_bf16)),
                                        _dot_nt(q, edge[:, sl])], axis=1))
        return out

    diff_scores = scores(qd_ref, ckd_ref, kd_edge)
    sb_scores = scores(qs_ref, cks_ref, ks_edge)
    yield

    lam = _lambda(lq1_ref[...], lk1_ref[...], lq2_ref[...], lk2_ref[...])
    probs, sums = [], []
    for s in diff_scores:
        s = jnp.where(mask_chunk, s, NEG_INF)
        p = jnp.exp2(s - jnp.max(s, axis=-1, keepdims=True))
        probs.append(p.astype(_bf16))
        sums.append(jnp.sum(p, axis=-1, keepdims=True))
    for h, (sl, p, l) in enumerate(zip(blocks, probs, sums)):
        v_cache = cvd_ref[0, pl.ds(h, past, stride=N_HEADS_DIFF), :].astype(_bf16)
        pv = _dot(p[:, :past], v_cache) + _dot(p[:, past:], vd_edge[:, sl])
        out = _diff_finish(pv[:ts], l[:ts], pv[ts:], l[ts:], lam, gh_ref[...])
        od_ref[:, sl] = out.astype(_bf16)
    yield

    n_tiles = past // LANES
    tiles = [slice(t * LANES, (t + 1) * LANES) for t in range(n_tiles)]
    edge_cols = slice(past, n_cols)
    log_betas, log_keeps, cache_pieces, edge_pieces = [], [], [], []
    for s in sb_scores:
        log_beta = _log2_sigmoid(s)
        log_keep = jnp.where(mask_strict, log_beta - s, 0.0)
        keep_hi = log_keep.astype(_bf16)
        keep_lo = (log_keep - keep_hi.astype(_f32)).astype(_bf16)
        log_betas.append(log_beta)
        log_keeps.append(log_keep)
        cache_pieces += [keep_hi[:, t] for t in tiles] + [keep_lo[:, t] for t in tiles]
        edge_pieces += [keep_hi[:, edge_cols], keep_lo[:, edge_cols]]
    after_cache = _dot(jnp.concatenate(cache_pieces, axis=0), _suffix_matrix(LANES))
    ej = lax.broadcasted_iota(jnp.int32, (LANES, LANES), 0)
    es = lax.broadcasted_iota(jnp.int32, (LANES, LANES), 1)
    edge_suffix = jnp.where(jnp.logical_and(ej > es, (ej < N_META) == (es < N_META)), 1.0, 0.0)
    after_edge = _dot(jnp.concatenate(edge_pieces, axis=0), edge_suffix.astype(_bf16))
    yield
    is_meta = lax.broadcasted_iota(jnp.int32, (grp, LANES), 1) < N_META
    for j, sl in enumerate(blocks):
        base = j * 2 * n_tiles
        after = [after_cache[(base + t) * grp:(base + t + 1) * grp]
                 + after_cache[(base + n_tiles + t) * grp:(base + n_tiles + t + 1) * grp]
                 for t in range(n_tiles)]
        edge_after = (after_edge[2 * j * grp:(2 * j + 1) * grp]
                      + after_edge[(2 * j + 1) * grp:(2 * j + 2) * grp])
        keep_edge = log_keeps[j][:, edge_cols]
        carry = edge_after[:, N_META:N_META + 1] + keep_edge[:, N_META:N_META + 1]
        shifted = [None] * n_tiles
        for t in reversed(range(n_tiles)):
            shifted[t] = after[t] + carry
            carry = carry + after[t][:, 0:1] + log_keeps[j][:, tiles[t]][:, 0:1]
        shifted.append(edge_after + jnp.where(is_meta, carry, 0.0))
        a = jnp.exp2(log_betas[j] + jnp.concatenate(shifted, axis=1))
        a = jnp.where(mask_strict, a, 0.0).astype(_bf16)
        pv = _dot_nt(a[:, :past], cvs_ref[0, sl, :].astype(_bf16)) + _dot(a[:, past:], vs_edge[:, sl])
        os_ref[:, sl] = jnp.where(lo, pv[:ts], pv[ts:]).astype(_bf16)


N_SAMP_INPUTS = 19


def _samp_call_parts(args, ts):
    assert len(args) == N_SAMP_INPUTS
    nb, _, past = args[10].shape
    assert past % LANES == 0 and N_META + ts <= LANES
    row_blk = pl.BlockSpec((ts, DIFF_WIDTH), lambda b: (b, 0))
    per_batch = lambda a: pl.BlockSpec((1,) + a.shape[1:], lambda b: (b, 0, 0))
    small = lambda a: pl.BlockSpec(a.shape, lambda b: (0, 0))
    out = jax.ShapeDtypeStruct((nb * ts, DIFF_WIDTH), _bf16)
    in_specs = ([row_blk] * 6 + [small(a) for a in args[6:10]] + [per_batch(a) for a in args[10:14]]
                + [small(a) for a in args[14:]])
    scratch = [pltpu.VMEM((LANES, DIFF_WIDTH), _bf16)] * 4
    return dict(ts=ts, past=past), nb, in_specs, [row_blk, row_blk], [out, out], scratch


def _samp_attn(args, ts):
    kwargs, nb, in_specs, out_specs, out_shape, scratch = _samp_call_parts(args, ts)
    return pl.pallas_call(
        functools.partial(_samp_kernel, **kwargs),
        grid=(nb,),
        in_specs=in_specs, out_specs=out_specs, out_shape=out_shape, scratch_shapes=scratch,
        compiler_params=pltpu.CompilerParams(
            dimension_semantics=("arbitrary",), vmem_limit_bytes=VMEM_LIMIT),
        name="samp_attn",
    )(*args)


def _mlp_kernel(*refs, tf):
    _run_phases(_mlp_phases(*refs, tf=tf))


def _mlp_phases(x_ref, md_ref, ms_ref, wo_ref, gm_ref, wu_ref, wd_ref, gf_ref, o_ref, h2_ref, *, tf):
    x1 = (x_ref[...] + _dot(md_ref[...], wo_ref[:DIFF_WIDTH, :])
          + _dot(ms_ref[...], wo_ref[DIFF_WIDTH:, :]))
    h2_ref[...] = _rms(x1, gm_ref[...]).astype(_bf16)
    o_ref[...] = x1
    yield
    for f in range(D_FF // tf):
        u = jnp.maximum(_dot(h2_ref[...], wu_ref[:, f * tf:(f + 1) * tf]), 0.0)
        o_ref[...] += _dot((u * u).astype(_bf16), wd_ref[f * tf:(f + 1) * tf, :])
        yield
    o_ref[...] = _rms(o_ref[...], gf_ref[...])


N_MLP_INPUTS = 8
MLP_ROWS = 512
MLP_FF_CHUNK = 1024


def _mlp_call_parts(args):
    assert len(args) == N_MLP_INPUTS
    x, _, _, wo, g_mlp, wu, wd, g_final = args
    rows, tm = x.shape[0], MLP_ROWS
    resident = lambda a: pl.BlockSpec(a.shape, lambda i: (0, 0), pipeline_mode=pl.Buffered(1))
    small = lambda a: pl.BlockSpec(a.shape, lambda i: (0, 0))
    row_blk = lambda width: pl.BlockSpec((tm, width), lambda i: (i, 0))
    in_specs = [row_blk(D_MODEL), row_blk(DIFF_WIDTH), row_blk(SB_WIDTH),
                resident(wo), small(g_mlp), resident(wu), resident(wd), small(g_final)]
    return (rows // tm, in_specs, row_blk(D_MODEL), jax.ShapeDtypeStruct((rows, D_MODEL), _f32),
            [pltpu.VMEM((tm, D_MODEL), _bf16)])


def _mlp(*args):
    steps, in_specs, out_spec, out_shape, scratch = _mlp_call_parts(args)
    return pl.pallas_call(
        functools.partial(_mlp_kernel, tf=MLP_FF_CHUNK),
        grid=(steps,),
        in_specs=in_specs, out_specs=out_spec, out_shape=out_shape, scratch_shapes=scratch,
        compiler_params=pltpu.CompilerParams(
            dimension_semantics=("arbitrary",), vmem_limit_bytes=VMEM_LIMIT),
        name="mlp",
    )(*args)


def _mlp_samp_kernel(*refs, samp_kwargs):
    n_in = N_MLP_INPUTS + N_SAMP_INPUTS
    mlp_in, samp_in = refs[:N_MLP_INPUTS], refs[N_MLP_INPUTS:n_in]
    y_ref, od_ref, os_ref, h2_ref = refs[n_in:n_in + 4]
    edges = refs[n_in + 4:]
    _run_phases(_samp_phases(*samp_in, od_ref, os_ref, *edges, **samp_kwargs),
                _mlp_phases(*mlp_in, y_ref, h2_ref, tf=MLP_FF_CHUNK))


def _mlp_and_samp_attn(mlp_args, samp_args, ts):
    steps, m_in, m_out, m_shape, m_scratch = _mlp_call_parts(mlp_args)
    kwargs, nb, s_in, s_out, s_shape, s_scratch = _samp_call_parts(samp_args, ts)
    if steps != nb:
        return (_mlp(*mlp_args),) + tuple(_samp_attn(samp_args, ts))
    return pl.pallas_call(
        functools.partial(_mlp_samp_kernel, samp_kwargs=kwargs),
        grid=(steps,),
        in_specs=m_in + s_in, out_specs=[m_out] + s_out, out_shape=[m_shape] + s_shape,
        scratch_shapes=m_scratch + s_scratch,
        compiler_params=pltpu.CompilerParams(
            dimension_semantics=("arbitrary",), vmem_limit_bytes=VMEM_LIMIT),
        name="mlp_samp_attn",
    )(*mlp_args, *samp_args)


def kernel(x_prompt, x_sample, cache_diff_k, cache_diff_v, cache_sb_k, cache_sb_v, meta_tokens, g_mix, w_in, lambda_q1, lambda_k1, lambda_q2, lambda_k2, g_diff_head, w_out, g_mlp, w_up, w_down, g_final):
    bp, tp, d = x_prompt.shape
    bs, ts, _ = x_sample.shape
    past = cache_diff_k.shape[2]
    assert bp == 1 and g_mix.shape[0] == 1, "single prompt sequence, depth 1"

    w_in_b = w_in[0].astype(_bf16)
    w_out_b = w_out[0].astype(_bf16)
    w_up_b = w_up[0].astype(_bf16)
    w_down_b = w_down[0].astype(_bf16)
    g_final2 = g_final[None, :]
    lams = (lambda_q1, lambda_k1, lambda_q2, lambda_k2)

    x_ms = jnp.concatenate([meta_tokens.astype(x_sample.dtype), x_sample.reshape(bs * ts, d)], axis=0)
    pos_ms = jnp.concatenate([jnp.arange(N_META), jnp.tile(N_META + past + jnp.arange(ts), bs)])
    cos_s, sin_s = _rope_tables(pos_ms)
    qd_s, qs_s, meta_f32, samp_f32, (kdm, ksm, vdm, vsm), (kdn, ksn, vdn, vsn) = _proj(
        x_ms, g_mix, w_in_b, cos_s, sin_s)

    p_out = _proj_prompt(x_prompt[0], g_mix, w_in_b, N_META, meta_f32, tm=512)
    qd, qs, kd, ks, vd, vs, kdb, ksb, vdb, vsb = p_out

    mixed_d = _diff_attn(qd, kdb, vdb, kdm, vdm, *lams, g_diff_head)
    mixed_s = _sb_attn(qs, ksb, vsb, ksm, vsm)

    feat_major = lambda c: jnp.transpose(c[0].reshape(bs, past, DIFF_WIDTH), (0, 2, 1))
    mlp_weights = (w_out_b, g_mlp, w_up_b, w_down_b, g_final2)
    samp_args = (qd_s, qs_s, kdn, ksn, vdn, vsn, kdm, ksm, vdm, vsm,
                 feat_major(cache_diff_k), feat_major(cache_sb_k),
                 cache_diff_v[0].reshape(bs, past * N_HEADS_DIFF, 2 * HEAD_DIM),
                 feat_major(cache_sb_v), *lams, g_diff_head)
    y_prompt, smix_d, smix_s = _mlp_and_samp_attn((x_prompt[0], mixed_d, mixed_s) + mlp_weights,
                                                  samp_args, ts)
    y_sample = _mlp(x_sample.reshape(bs * ts, d), smix_d, smix_s, *mlp_weights)

    def prompt_cache(a, tail):
        return a.reshape((1, bp, N_META + tp) + tail)

    def sample_cache(a, tail):
        return a.reshape((1, bs, ts) + tail)

    dk_tail, dv_tail = (N_HEADS_DIFF, 2, HEAD_DIM), (N_HEADS_DIFF, 2 * HEAD_DIM)
    sb_tail = (N_HEADS_SB, HEAD_DIM)
    return (y_prompt[None], y_sample.reshape(bs, ts, d),
            prompt_cache(kd, dk_tail), prompt_cache(vd, dv_tail),
            prompt_cache(ks, sb_tail), prompt_cache(vs, sb_tail),
            sample_cache(samp_f32[0], dk_tail), sample_cache(samp_f32[2], dv_tail),
            sample_cache(samp_f32[1], sb_tail), sample_cache(samp_f32[3], sb_tail))
```

```python
import functools
import math

import jax
import jax.numpy as jnp
from jax import lax
from jax.experimental import pallas as pl
from jax.experimental.pallas import tpu as pltpu

D_MODEL = 1024
CHUNK = 64
CHUNK_SHIFT = 6
N_META = 16
HEAD_DIM = 64
N_HEADS_DIFF = 4
N_HEADS_SB = 8
DIFF_WIDTH = N_HEADS_DIFF * 2 * HEAD_DIM
SB_WIDTH = N_HEADS_SB * HEAD_DIM
D_FF = 4 * D_MODEL
ROPE_THETA = 10000.0
NORM_EPS = 1e-6
NEG_INF = -1e30
LAMBDA_INIT = 0.8 - 0.6 * math.exp(-0.3 * 0)
LOG2_E = math.log2(math.e)

LANES = 128
META_PAD = LANES
SB_LOG2_STOP = -44.0
SCORE_LEAD = 3

VMEM_LIMIT = 56 * 1024 * 1024

_f32 = jnp.float32
_bf16 = jnp.bfloat16


def _dot(a, b):
    return jnp.dot(a, b, preferred_element_type=_f32)


def _dot_nt(a, b):
    return lax.dot_general(a, b, (((1,), (1,)), ((), ())), preferred_element_type=_f32)


def _rms(x, g):
    return x * lax.rsqrt(jnp.mean(x * x, axis=-1, keepdims=True) + NORM_EPS) * g


def _log2_sigmoid(s2):
    return jnp.minimum(s2, 0.0) - jnp.log2(1.0 + jnp.exp2(-jnp.abs(s2)))


def _first_half_lanes(shape):
    lane = lax.broadcasted_iota(jnp.int32, shape, 1)
    return (lane & (HEAD_DIM - 1)) < (HEAD_DIM // 2)


def _proj_math(x_ref, g_ref, w_ref, cos, sin):
    h = _rms(x_ref[...], g_ref[...]).astype(_bf16)
    first_half = _first_half_lanes(cos.shape)

    def rope(z):
        parts = []
        for j in range(z.shape[1] // LANES):
            zj = z[:, j * LANES:(j + 1) * LANES]
            swapped = jnp.where(first_half,
                                pltpu.roll(zj, LANES - HEAD_DIM // 2, 1),
                                pltpu.roll(zj, HEAD_DIM // 2, 1))
            parts.append(zj * cos + swapped * sin)
        return jnp.concatenate(parts, axis=1)

    def cols(c):
        return _dot(h, w_ref[:, c * DIFF_WIDTH:(c + 1) * DIFF_WIDTH])

    scale = HEAD_DIM ** -0.5
    scale = scale * LOG2_E
    return (rope(cols(0)) * scale, cols(1) * scale, rope(cols(2)), cols(3), cols(4), cols(5))


def _proj_kernel(x_ref, g_ref, w_ref, cos_ref, sin_ref, qd_ref, qs_ref, *kv_refs):
    qd, qs, kd, ks, vd, vs = _proj_math(x_ref, g_ref, w_ref, cos_ref[...], sin_ref[...])
    qd_ref[...] = qd[N_META:].astype(_bf16)
    qs_ref[...] = qs[N_META:].astype(_bf16)
    n = len(kv_refs) // 4
    meta_f32, samp_f32, meta_b16, samp_b16 = (kv_refs[i * n:(i + 1) * n] for i in range(4))
    for j, val in enumerate((kd, ks, vd, vs)):
        meta_f32[j][...] = val[:N_META]
        samp_f32[j][...] = val[N_META:]
        b16 = val.astype(_bf16)
        meta_b16[j][:N_META, :] = b16[:N_META]
        meta_b16[j][N_META:, :] = jnp.zeros((META_PAD - N_META, DIFF_WIDTH), _bf16)
        samp_b16[j][...] = b16[N_META:]


def _proj(x, g, w_bf16, cos, sin):
    rows = x.shape[0] - N_META
    full = lambda a: pl.BlockSpec(a.shape, lambda i: (0, 0))
    out = lambda r, dtype: jax.ShapeDtypeStruct((r, DIFF_WIDTH), dtype)
    out_shape = ([out(rows, _bf16)] * 2 + [out(N_META, _f32)] * 4 + [out(rows, _f32)] * 4
                 + [out(META_PAD, _bf16)] * 4 + [out(rows, _bf16)] * 4)
    res = pl.pallas_call(
        _proj_kernel,
        grid=(1,),
        in_specs=[full(x), full(g), full(w_bf16), full(cos), full(sin)],
        out_specs=[full(o) for o in out_shape],
        out_shape=out_shape,
        compiler_params=pltpu.CompilerParams(
            dimension_semantics=("arbitrary",), vmem_limit_bytes=VMEM_LIMIT),
        name="proj",
    )(x, g, w_bf16, cos, sin)
    return res[0], res[1], res[2:6], res[6:10], res[10:14], res[14:18]


def _proj_prompt_kernel(x_ref, g_ref, w_ref, cos_row_ref, sin_row_ref, cos_blk_ref, sin_blk_ref,
                        mkd_ref, mks_ref, mvd_ref, mvs_ref,
                        qd_ref, qs_ref, kd_ref, ks_ref, vd_ref, vs_ref,
                        kdb_ref, ksb_ref, vdb_ref, vsb_ref, tail_ref, *, tm):
    i = pl.program_id(0)
    last = pl.num_programs(0) - 1
    row_outs = ((0, kd_ref), (1, ks_ref), (3, vs_ref))

    def put_vd(token0, rows):
        for h in range(N_HEADS_DIFF):
            vd_ref[pl.ds(token0 * N_HEADS_DIFF + h, rows.shape[0], stride=N_HEADS_DIFF), :] = (
                rows[:, h * LANES:(h + 1) * LANES])

    @pl.when(i == 0)
    def _():
        for j, meta in enumerate((mkd_ref, mks_ref, mvd_ref, mvs_ref)):
            tail_ref[j] = meta[...]

    @pl.when(i < last)
    def _():
        cos_r, sin_r, cos_b, sin_b = cos_row_ref[...], sin_row_ref[...], cos_blk_ref[0], sin_blk_ref[0]
        cos = cos_b * cos_r - sin_b * sin_r
        sin = sin_b * cos_r + cos_b * sin_r
        sin = jnp.where(_first_half_lanes(sin.shape), -sin, sin)
        qd, qs, kd, ks, vd, vs = _proj_math(x_ref, g_ref, w_ref, cos, sin)
        qd_ref[...] = qd.astype(_bf16)
        qs_ref[...] = qs.astype(_bf16)
        for b16_ref, val in ((kdb_ref, kd), (ksb_ref, ks), (vdb_ref, vd), (vsb_ref, vs)):
            b16_ref[...] = val.astype(_bf16)
        for (j, out), val in zip(row_outs, (kd, ks, vs)):
            out[:N_META, :] = tail_ref[j]
            out[N_META:, :] = val[:tm - N_META]
            tail_ref[j] = val[tm - N_META:]
        put_vd(0, tail_ref[2])
        put_vd(N_META, vd[:tm - N_META])
        tail_ref[2] = vd[tm - N_META:]

    @pl.when(i == last)
    def _():
        for j, out in row_outs:
            out[:N_META, :] = tail_ref[j]
        put_vd(0, tail_ref[2])


def _proj_prompt(x, g, w_bf16, pos0, meta_kv, tm):
    rows = x.shape[0]
    nblk = rows // tm
    cos_row, sin_row = _angle_tables(jnp.arange(tm))
    cos_blk, sin_blk = [t[:, None, :] for t in _angle_tables(pos0 + tm * jnp.arange(nblk))]
    frame_blk = lambda width: pl.BlockSpec((tm, width), lambda i: (jnp.minimum(i, nblk - 1), 0))
    blk_angle = pl.BlockSpec((1, 1, LANES), lambda i: (jnp.minimum(i, nblk - 1), 0, 0))
    full = lambda a: pl.BlockSpec(a.shape, lambda i: (0, 0))
    f32_out = jax.ShapeDtypeStruct((N_META + rows, DIFF_WIDTH), _f32)
    vd_out = jax.ShapeDtypeStruct(((N_META + rows) * N_HEADS_DIFF, LANES), _f32)
    b16_out = jax.ShapeDtypeStruct((rows, DIFF_WIDTH), _bf16)
    stream_blk = pl.BlockSpec((tm, DIFF_WIDTH), lambda i: (i, 0))
    return pl.pallas_call(
        functools.partial(_proj_prompt_kernel, tm=tm),
        grid=(nblk + 1,),
        in_specs=[frame_blk(D_MODEL), full(g), full(w_bf16), full(cos_row), full(sin_row),
                  blk_angle, blk_angle] + [full(m) for m in meta_kv],
        out_specs=[frame_blk(DIFF_WIDTH)] * 2
                  + [stream_blk, stream_blk,
                     pl.BlockSpec((tm * N_HEADS_DIFF, LANES), lambda i: (i, 0)), stream_blk]
                  + [frame_blk(DIFF_WIDTH)] * 4,
        out_shape=[b16_out, b16_out, f32_out, f32_out, vd_out, f32_out,
                   b16_out, b16_out, b16_out, b16_out],
        scratch_shapes=[pltpu.VMEM((4, N_META, DIFF_WIDTH), _f32)],
        compiler_params=pltpu.CompilerParams(
            dimension_semantics=("arbitrary",), vmem_limit_bytes=VMEM_LIMIT),
        name="proj_prompt",
    )(x, g, w_bf16, cos_row, sin_row, cos_blk, sin_blk, *meta_kv)


def _angle_tables(pos):
    half = HEAD_DIM // 2
    inv_freq = ROPE_THETA ** (-jnp.arange(half, dtype=_f32) / half)
    ang = pos.astype(_f32)[:, None] * inv_freq[None, :]
    reps = LANES // half
    return jnp.tile(jnp.cos(ang), (1, reps)), jnp.tile(jnp.sin(ang), (1, reps))


def _rope_tables(pos):
    cos, sin = _angle_tables(pos)
    return cos, jnp.where(_first_half_lanes(sin.shape), -sin, sin)


def _lane_half_masks(shape):
    lane = lax.broadcasted_iota(jnp.int32, shape, 1)
    return lane < HEAD_DIM, lane >= HEAD_DIM


def _lambda(lq1, lk1, lq2, lk2):
    return (jnp.exp(jnp.sum(lq1 * lk1, axis=-1, keepdims=True))
            - jnp.exp(jnp.sum(lq2 * lk2, axis=-1, keepdims=True)) + LAMBDA_INIT)


def _diff_finish(o0, l0, o1, l1, lam, g_head):
    o = o0 / l0 - lam * (o1 / l1)
    return _rms(o, g_head) * (1.0 - LAMBDA_INIT)


def _diff_kernel(q_ref, k_ref, v_ref, km_ref, vm_ref, lq1_ref, lk1_ref, lq2_ref, lk2_ref, gh_ref,
                 o_ref, m_ref, acc_ref, *, bq, bk, heads):
    qi = pl.program_id(1)
    lo, hi = _lane_half_masks((bq, LANES))
    zero = jnp.zeros((bq, LANES), _bf16)
    chains = []
    for a in range(heads):
        sl = slice(a * LANES, (a + 1) * LANES)
        q = q_ref[:, sl]
        chains.append((2 * a, sl, jnp.where(lo, q, zero)))
        chains.append((2 * a + 1, sl, jnp.where(hi, q, zero)))

    def softmax_pv(c, s, v, mask, first):
        if mask is not None:
            s = jnp.where(mask, s, NEG_INF)
        m_cur = jnp.max(s, axis=-1, keepdims=True)
        m_new = jnp.broadcast_to(m_cur, (bq, LANES)) if first else jnp.maximum(m_ref[c], m_cur)
        p = jnp.exp2((s - jnp.tile(m_new, (1, s.shape[1] // LANES))).astype(_bf16))
        pv = _dot(p, jnp.concatenate([v, jnp.ones_like(v)], axis=1))
        if first:
            acc_ref[c] = pv
        else:
            alpha = jnp.exp2(m_ref[c] - m_new)
            acc_ref[c] = jnp.tile(alpha, (1, 2)) * acc_ref[c] + pv
        m_ref[c] = m_new

    def step(k_of, v_of, mask, first):
        scores = {}
        for i in range(len(chains) + SCORE_LEAD):
            if i < len(chains):
                _, sl, qz = chains[i]
                scores[i] = _dot_nt(qz, k_of(sl))
            if i >= SCORE_LEAD:
                c, sl, _ = chains[i - SCORE_LEAD]
                softmax_pv(c, scores.pop(i - SCORE_LEAD), v_of(sl), mask, first)

    q0 = qi * bq
    n_full = (q0 + CHUNK) // bk

    start = pl.multiple_of(n_full * bk, bk)
    row = lax.broadcasted_iota(jnp.int32, (bq, bk + META_PAD), 0) + q0
    col = lax.broadcasted_iota(jnp.int32, (bq, bk + META_PAD), 1)
    key_chunk = jnp.where(col < bk, (col + start) >> CHUNK_SHIFT,
                          jnp.where(col < bk + N_META, -1, jnp.iinfo(jnp.int32).max))
    mask = key_chunk <= (row >> CHUNK_SHIFT)
    step(lambda sl: jnp.concatenate([k_ref[pl.ds(start, bk), sl], km_ref[:, sl]], axis=0),
         lambda sl: jnp.concatenate([v_ref[pl.ds(start, bk), sl], vm_ref[:, sl]], axis=0), mask, True)

    def full_keys(start, width):
        start = pl.multiple_of(start, bk)
        step(lambda sl: k_ref[pl.ds(start, width), sl], lambda sl: v_ref[pl.ds(start, width), sl],
             None, False)

    def tile_pair(j, carry):
        full_keys(j * 2 * bk, 2 * bk)
        return carry

    lax.fori_loop(0, n_full // 2, tile_pair, 0)

    @pl.when(n_full % 2 == 1)
    def _():
        full_keys((n_full - 1) * bk, bk)

    lam = _lambda(lq1_ref[...], lk1_ref[...], lq2_ref[...], lk2_ref[...])
    for a in range(heads):
        acc0, acc1 = acc_ref[2 * a], acc_ref[2 * a + 1]
        out = _diff_finish(acc0[:, :LANES], acc0[:, LANES:], acc1[:, :LANES], acc1[:, LANES:],
                           lam, gh_ref[...])
        o_ref[:, a * LANES:(a + 1) * LANES] = out.astype(_bf16)


def _diff_attn(qd, kdb, vdb, kmeta, vmeta, lq1, lk1, lq2, lk2, g_head, bq=512, bk=512,
               heads=N_HEADS_DIFF):
    n = qd.shape[0]
    width = heads * LANES
    head_blk = lambda rows: pl.BlockSpec((rows, width), lambda h, i: (0, h),
                                         pipeline_mode=pl.Buffered(1))
    small = lambda a: pl.BlockSpec(a.shape, lambda h, i: (0, 0))
    return pl.pallas_call(
        functools.partial(_diff_kernel, bq=bq, bk=bk, heads=heads),
        grid=(N_HEADS_DIFF // heads, n // bq),
        in_specs=[pl.BlockSpec((bq, width), lambda h, i: (i, h)),
                  head_blk(n), head_blk(n), head_blk(META_PAD), head_blk(META_PAD),
                  small(lq1), small(lk1), small(lq2), small(lk2), small(g_head)],
        out_specs=pl.BlockSpec((bq, width), lambda h, i: (i, h)),
        out_shape=jax.ShapeDtypeStruct((n, DIFF_WIDTH), _bf16),
        scratch_shapes=[pltpu.VMEM((2 * heads, bq, LANES), _f32),
                        pltpu.VMEM((2 * heads, bq, 2 * LANES), _f32)],
        compiler_params=pltpu.CompilerParams(
            dimension_semantics=("arbitrary", "arbitrary"), vmem_limit_bytes=VMEM_LIMIT),
        name="diff_attn",
    )(qd, kdb, vdb, kmeta, vmeta, lq1, lk1, lq2, lk2, g_head)


def _suffix_matrix(n):
    j = lax.broadcasted_iota(jnp.int32, (n, n), 0)
    s = lax.broadcasted_iota(jnp.int32, (n, n), 1)
    return jnp.where(j > s, 1.0, 0.0).astype(_bf16)


def _sb_tile_phases(queries, keys, values, carries, mask, suffix, out):
    masks = mask if isinstance(mask, (list, tuple)) else [mask] * len(queries)
    scores = [_dot_nt(q, k) for q, k in zip(queries, keys)]
    yield
    log_betas, log_keeps = [], []
    for s, m in zip(scores, masks):
        log_beta = _log2_sigmoid(s)
        log_keep = log_beta - s
        if m is not None:
            log_keep = jnp.where(m, log_keep, 0.0)
        log_betas.append(log_beta)
        log_keeps.append(log_keep)
    afters = [_dot(log_keep.astype(_bf16), suffix) for log_keep in log_keeps]
    yield
    weights, new_carries = [], []
    for log_beta, log_keep, after, carry, m in zip(log_betas, log_keeps, afters, carries, masks):
        a = jnp.exp2(log_beta + after + carry)
        if m is not None:
            a = jnp.where(m, a, 0.0)
        weights.append(a.astype(_bf16))
        new_carries.append(carry + after[:, 0:1] + log_keep[:, 0:1])
    out.append(([_dot(a, v) for a, v in zip(weights, values)], new_carries))


def _sb_kernel(q_ref, k_ref, v_ref, km_ref, vm_ref, o_ref, acc_ref, carry_ref, live_ref, *, bq):
    qi = pl.program_id(0)
    lo, hi = _lane_half_masks((bq, LANES))
    zero = jnp.zeros((bq, LANES), _bf16)
    pairs = [slice(p * LANES, (p + 1) * LANES) for p in range(SB_WIDTH // LANES)]
    lanes = [sl for sl in pairs for _ in range(2)]
    heads = range(N_HEADS_SB)
    bk = 2 * bq
    queries = []
    for sl in pairs:
        q = q_ref[:, sl]
        queries += [jnp.where(lo, q, zero), jnp.where(hi, q, zero)]

    def older_tile(keys, values, mask, width):
        result = []
        _run_phases(_sb_tile_phases(queries, keys, values, [carry_ref[c] for c in heads], mask,
                                    _suffix_matrix(width), result))
        pvs, carries = result[0]
        live = carries[0]
        for c in heads:
            acc_ref[c] += pvs[c]
            carry_ref[c] = carries[c]
            live = jnp.maximum(live, carries[c])
        return jnp.max(live)

    def frame_tile(start, width):
        start = pl.multiple_of(start, bq)
        return older_tile([k_ref[pl.ds(start, width), sl] for sl in lanes],
                          [v_ref[pl.ds(start, width), sl] for sl in lanes], None, width)

    def half_tiles(starts, masks, first):
        qs2, ks2, vs2, carries = [], [], [], []
        for h, rows in enumerate((slice(0, half), slice(half, bq))):
            start = pl.multiple_of(starts[h], half)
            for c in heads:
                qs2.append(queries[c][rows])
                ks2.append(k_ref[pl.ds(start, bq), lanes[c]])
                vs2.append(v_ref[pl.ds(start, bq), lanes[c]])
                carries.append(jnp.zeros((half, 1), _f32) if first else carry_ref[c, rows])
        result = []
        _run_phases(_sb_tile_phases(qs2, ks2, vs2, carries, [m for m in masks for _ in heads],
                                    _suffix_matrix(bq), result))
        pvs, carries = result[0]
        live = carries[0]
        for h, rows in enumerate((slice(0, half), slice(half, bq))):
            for c in heads:
                i = h * N_HEADS_SB + c
                acc_ref[c, rows] = pvs[i] if first else acc_ref[c, rows] + pvs[i]
                carry_ref[c, rows] = carries[i]
                live = jnp.maximum(live, carries[i])
        return jnp.max(live)

    q0 = qi * bq
    half = bq // 2
    start_a = jnp.maximum(q0 - half, 0)
    row = lax.broadcasted_iota(jnp.int32, (half, bq), 0) + q0
    col = lax.broadcasted_iota(jnp.int32, (half, bq), 1)
    live_ref[0] = half_tiles((start_a, q0), (col + start_a < row, col + q0 < row + half), True)

    @pl.when(jnp.logical_and(q0 > 0, live_ref[0] > SB_LOG2_STOP))
    def _():
        live_ref[0] = half_tiles((q0 - bq, q0 - bq), (col < half, None), False)

    start0 = jnp.maximum(q0 - bq, 0)

    def cond(state):
        start, live = state
        return jnp.logical_and(start >= bk, live > SB_LOG2_STOP)

    def body(state):
        start, _ = state
        return start - bk, frame_tile(start - bk, bk)

    start, live = lax.while_loop(cond, body, (start0, live_ref[0]))
    live_ref[0] = live

    @pl.when(jnp.logical_and(start == bq, live > SB_LOG2_STOP))
    def _():
        live_ref[0] = frame_tile(0, bq)

    @pl.when(jnp.logical_and(start < bk, live_ref[0] > SB_LOG2_STOP))
    def _():
        mcol = lax.broadcasted_iota(jnp.int32, (bq, META_PAD), 1)
        older_tile([km_ref[:, sl] for sl in lanes], [vm_ref[:, sl] for sl in lanes], mcol < N_META,
                   META_PAD)

    for p, sl in enumerate(pairs):
        o_ref[:, sl] = jnp.where(lo, acc_ref[2 * p], acc_ref[2 * p + 1]).astype(_bf16)


def _sb_attn(qs, ksb, vsb, kmeta, vmeta, bq=128):
    n = qs.shape[0]
    resident = lambda a: pl.BlockSpec(a.shape, lambda i: (0, 0), pipeline_mode=pl.Buffered(1))
    return pl.pallas_call(
        functools.partial(_sb_kernel, bq=bq),
        grid=(n // bq,),
        in_specs=[pl.BlockSpec((bq, SB_WIDTH), lambda i: (i, 0)),
                  resident(ksb), resident(vsb), resident(kmeta), resident(vmeta)],
        out_specs=pl.BlockSpec((bq, SB_WIDTH), lambda i: (i, 0)),
        out_shape=jax.ShapeDtypeStruct((n, SB_WIDTH), _bf16),
        scratch_shapes=[pltpu.VMEM((N_HEADS_SB, bq, LANES), _f32),
                        pltpu.VMEM((N_HEADS_SB, bq, 1), _f32),
                        pltpu.SMEM((1,), _f32)],
        compiler_params=pltpu.CompilerParams(
            dimension_semantics=("arbitrary",), vmem_limit_bytes=VMEM_LIMIT),
        name="sb_attn",
    )(qs, ksb, vsb, kmeta, vmeta)


def _run_phases(*phase_generators):
    pending = list(phase_generators)
    while pending:
        pending = [g for g in pending if next(g, "done") != "done"]


def _samp_phases(qd_ref, qs_ref, kdn_ref, ksn_ref, vdn_ref, vsn_ref,
                 kdm_ref, ksm_ref, vdm_ref, vsm_ref,
                 ckd_ref, cks_ref, cvd_ref, cvs_ref,
                 lq1_ref, lk1_ref, lq2_ref, lk2_ref, gh_ref,
                 od_ref, os_ref, kd_edge, ks_edge, vd_edge, vs_edge, *, ts, past):
    for edge, meta, new in ((kd_edge, kdm_ref, kdn_ref), (ks_edge, ksm_ref, ksn_ref),
                            (vd_edge, vdm_ref, vdn_ref), (vs_edge, vsm_ref, vsn_ref)):
        edge[0:N_META, :] = meta[0:N_META, :]
        edge[N_META:N_META + ts, :] = new[...]
        edge[N_META + ts:, :] = jnp.zeros((LANES - N_META - ts, DIFF_WIDTH), _bf16)

    grp = 2 * ts
    n_cols = past + LANES
    col = lax.broadcasted_iota(jnp.int32, (grp, n_cols), 1)
    ecol = col - past
    pos_k = jnp.where(ecol < 0, col + N_META, jnp.where(ecol < N_META, ecol, ecol + past))
    pos_q = (lax.broadcasted_iota(jnp.int32, (grp, n_cols), 0) & (ts - 1)) + N_META + past
    valid = ecol < N_META + ts
    chunk = lambda p: jnp.where(p < N_META, -1, (p - N_META) >> CHUNK_SHIFT)
    mask_chunk = jnp.logical_and(valid, chunk(pos_k) <= chunk(pos_q))
    mask_strict = jnp.logical_and(valid, pos_k < pos_q)

    lo, hi = _lane_half_masks((ts, LANES))
    zero = jnp.zeros((ts, LANES), _bf16)
    blocks = [slice(j * LANES, (j + 1) * LANES) for j in range(DIFF_WIDTH // LANES)]

    def stacked(q):
        return jnp.concatenate([jnp.where(lo, q, zero), jnp.where(hi, q, zero)], axis=0)

    def scores(q_ref, cache_ref, edge):
        out = []
        for sl in blocks:
            q = stacked(q_ref[:, sl])
            out.append(jnp.concatenate([_dot(q, cache_ref[0, sl, :].astype(_bf16)),
                                        _dot_nt(q, edge[:, sl])], axis=1))
        return out

    diff_scores = scores(qd_ref, ckd_ref, kd_edge)
    sb_scores = scores(qs_ref, cks_ref, ks_edge)
    yield

    lam = _lambda(lq1_ref[...], lk1_ref[...], lq2_ref[...], lk2_ref[...])
    probs, sums = [], []
    for s in diff_scores:
        s = jnp.where(mask_chunk, s, NEG_INF)
        p = jnp.exp2(s - jnp.max(s, axis=-1, keepdims=True))
        probs.append(p.astype(_bf16))
        sums.append(jnp.sum(p, axis=-1, keepdims=True))
    for h, (sl, p, l) in enumerate(zip(blocks, probs, sums)):
        v_cache = cvd_ref[0, pl.ds(h, past, stride=N_HEADS_DIFF), :].astype(_bf16)
        pv = _dot(p[:, :past], v_cache) + _dot(p[:, past:], vd_edge[:, sl])
        out = _diff_finish(pv[:ts], l[:ts], pv[ts:], l[ts:], lam, gh_ref[...])
        od_ref[:, sl] = out.astype(_bf16)
    yield

    n_tiles = past // LANES
    tiles = [slice(t * LANES, (t + 1) * LANES) for t in range(n_tiles)]
    edge_cols = slice(past, n_cols)
    log_betas, log_keeps, cache_pieces, edge_pieces = [], [], [], []
    for s in sb_scores:
        log_beta = _log2_sigmoid(s)
        log_keep = jnp.where(mask_strict, log_beta - s, 0.0)
        keep_hi = log_keep.astype(_bf16)
        keep_lo = (log_keep - keep_hi.astype(_f32)).astype(_bf16)
        log_betas.append(log_beta)
        log_keeps.append(log_keep)
        cache_pieces += [keep_hi[:, t] for t in tiles] + [keep_lo[:, t] for t in tiles]
        edge_pieces += [keep_hi[:, edge_cols], keep_lo[:, edge_cols]]
    after_cache = _dot(jnp.concatenate(cache_pieces, axis=0), _suffix_matrix(LANES))
    ej = lax.broadcasted_iota(jnp.int32, (LANES, LANES), 0)
    es = lax.broadcasted_iota(jnp.int32, (LANES, LANES), 1)
    edge_suffix = jnp.where(jnp.logical_and(ej > es, (ej < N_META) == (es < N_META)), 1.0, 0.0)
    after_edge = _dot(jnp.concatenate(edge_pieces, axis=0), edge_suffix.astype(_bf16))
    yield
    is_meta = lax.broadcasted_iota(jnp.int32, (grp, LANES), 1) < N_META
    for j, sl in enumerate(blocks):
        base = j * 2 * n_tiles
        after = [after_cache[(base + t) * grp:(base + t + 1) * grp]
                 + after_cache[(base + n_tiles + t) * grp:(base + n_tiles + t + 1) * grp]
                 for t in range(n_tiles)]
        edge_after = (after_edge[2 * j * grp:(2 * j + 1) * grp]
                      + after_edge[(2 * j + 1) * grp:(2 * j + 2) * grp])
        keep_edge = log_keeps[j][:, edge_cols]
        carry = edge_after[:, N_META:N_META + 1] + keep_edge[:, N_META:N_META + 1]
        shifted = [None] * n_tiles
        for t in reversed(range(n_tiles)):
            shifted[t] = after[t] + carry
            carry = carry + after[t][:, 0:1] + log_keeps[j][:, tiles[t]][:, 0:1]
        shifted.append(edge_after + jnp.where(is_meta, carry, 0.0))
        a = jnp.exp2(log_betas[j] + jnp.concatenate(shifted, axis=1))
        a = jnp.where(mask_strict, a, 0.0).astype(_bf16)
        pv = _dot_nt(a[:, :past], cvs_ref[0, sl, :].astype(_bf16)) + _dot(a[:, past:], vs_edge[:, sl])
        os_ref[:, sl] = jnp.where(lo, pv[:ts], pv[ts:]).astype(_bf16)


N_SAMP_INPUTS = 19


def _samp_call_parts(args, ts):
    assert len(args) == N_SAMP_INPUTS
    nb, _, past = args[10].shape
    assert past % LANES == 0 and N_META + ts <= LANES
    row_blk = pl.BlockSpec((ts, DIFF_WIDTH), lambda b: (b, 0))
    per_batch = lambda a: pl.BlockSpec((1,) + a.shape[1:], lambda b: (b, 0, 0))
    small = lambda a: pl.BlockSpec(a.shape, lambda b: (0, 0))
    out = jax.ShapeDtypeStruct((nb * ts, DIFF_WIDTH), _bf16)
    in_specs = ([row_blk] * 6 + [small(a) for a in args[6:10]] + [per_batch(a) for a in args[10:14]]
                + [small(a) for a in args[14:]])
    scratch = [pltpu.VMEM((LANES, DIFF_WIDTH), _bf16)] * 4
    return dict(ts=ts, past=past), nb, in_specs, [row_blk, row_blk], [out, out], scratch


def _mlp_kernel(*refs, tf):
    _run_phases(_mlp_phases(*refs, tf=tf))


def _mlp_phases(x_ref, md_ref, ms_ref, wo_ref, gm_ref, wu_ref, wd_ref, gf_ref, o_ref, h2_ref, *, tf):
    x1 = (x_ref[...] + _dot(md_ref[...], wo_ref[:DIFF_WIDTH, :])
          + _dot(ms_ref[...], wo_ref[DIFF_WIDTH:, :]))
    h2_ref[...] = _rms(x1, gm_ref[...]).astype(_bf16)
    o_ref[...] = x1
    yield
    for f in range(D_FF // tf):
        u = jnp.maximum(_dot(h2_ref[...], wu_ref[:, f * tf:(f + 1) * tf]), 0.0)
        o_ref[...] += _dot((u * u).astype(_bf16), wd_ref[f * tf:(f + 1) * tf, :])
        yield
    o_ref[...] = _rms(o_ref[...], gf_ref[...])


N_MLP_INPUTS = 8
MLP_ROWS = 512
MLP_FF_CHUNK = 1024


def _mlp_call_parts(args):
    assert len(args) == N_MLP_INPUTS
    x, _, _, wo, g_mlp, wu, wd, g_final = args
    rows, tm = x.shape[0], MLP_ROWS
    resident = lambda a: pl.BlockSpec(a.shape, lambda i: (0, 0), pipeline_mode=pl.Buffered(1))
    small = lambda a: pl.BlockSpec(a.shape, lambda i: (0, 0))
    row_blk = lambda width: pl.BlockSpec((tm, width), lambda i: (i, 0))
    in_specs = [row_blk(D_MODEL), row_blk(DIFF_WIDTH), row_blk(SB_WIDTH),
                resident(wo), small(g_mlp), resident(wu), resident(wd), small(g_final)]
    return (rows // tm, in_specs, row_blk(D_MODEL), jax.ShapeDtypeStruct((rows, D_MODEL), _f32),
            [pltpu.VMEM((tm, D_MODEL), _bf16)])


def _mlp(*args):
    steps, in_specs, out_spec, out_shape, scratch = _mlp_call_parts(args)
    return pl.pallas_call(
        functools.partial(_mlp_kernel, tf=MLP_FF_CHUNK),
        grid=(steps,),
        in_specs=in_specs, out_specs=out_spec, out_shape=out_shape, scratch_shapes=scratch,
        compiler_params=pltpu.CompilerParams(
            dimension_semantics=("arbitrary",), vmem_limit_bytes=VMEM_LIMIT),
        name="mlp",
    )(*args)


def _mlp_samp_kernel(*refs, samp_kwargs):
    n_in = N_MLP_INPUTS + N_SAMP_INPUTS
    mlp_in, samp_in = refs[:N_MLP_INPUTS], refs[N_MLP_INPUTS:n_in]
    y_ref, od_ref, os_ref, h2_ref = refs[n_in:n_in + 4]
    edges = refs[n_in + 4:]
    _run_phases(_samp_phases(*samp_in, od_ref, os_ref, *edges, **samp_kwargs),
                _mlp_phases(*mlp_in, y_ref, h2_ref, tf=MLP_FF_CHUNK))


def _mlp_and_samp_attn(mlp_args, samp_args, ts):
    steps, m_in, m_out, m_shape, m_scratch = _mlp_call_parts(mlp_args)
    kwargs, nb, s_in, s_out, s_shape, s_scratch = _samp_call_parts(samp_args, ts)
    assert steps == nb, "prompt MLP row blocks and decode batch elements must pair up one to one"
    return pl.pallas_call(
        functools.partial(_mlp_samp_kernel, samp_kwargs=kwargs),
        grid=(steps,),
        in_specs=m_in + s_in, out_specs=[m_out] + s_out, out_shape=[m_shape] + s_shape,
        scratch_shapes=m_scratch + s_scratch,
        compiler_params=pltpu.CompilerParams(
            dimension_semantics=("arbitrary",), vmem_limit_bytes=VMEM_LIMIT),
        name="mlp_samp_attn",
    )(*mlp_args, *samp_args)


def kernel(x_prompt, x_sample, cache_diff_k, cache_diff_v, cache_sb_k, cache_sb_v, meta_tokens, g_mix, w_in, lambda_q1, lambda_k1, lambda_q2, lambda_k2, g_diff_head, w_out, g_mlp, w_up, w_down, g_final):
    bp, tp, d = x_prompt.shape
    bs, ts, _ = x_sample.shape
    past = cache_diff_k.shape[2]
    assert bp == 1 and g_mix.shape[0] == 1, "single prompt sequence, depth 1"

    w_in_b = w_in[0].astype(_bf16)
    w_out_b = w_out[0].astype(_bf16)
    w_up_b = w_up[0].astype(_bf16)
    w_down_b = w_down[0].astype(_bf16)
    g_final2 = g_final[None, :]
    lams = (lambda_q1, lambda_k1, lambda_q2, lambda_k2)

    x_ms = jnp.concatenate([meta_tokens.astype(x_sample.dtype), x_sample.reshape(bs * ts, d)], axis=0)
    pos_ms = jnp.concatenate([jnp.arange(N_META), jnp.tile(N_META + past + jnp.arange(ts), bs)])
    cos_s, sin_s = _rope_tables(pos_ms)
    qd_s, qs_s, meta_f32, samp_f32, (kdm, ksm, vdm, vsm), (kdn, ksn, vdn, vsn) = _proj(
        x_ms, g_mix, w_in_b, cos_s, sin_s)

    p_out = _proj_prompt(x_prompt[0], g_mix, w_in_b, N_META, meta_f32, tm=512)
    qd, qs, kd, ks, vd, vs, kdb, ksb, vdb, vsb = p_out

    mixed_d = _diff_attn(qd, kdb, vdb, kdm, vdm, *lams, g_diff_head)
    mixed_s = _sb_attn(qs, ksb, vsb, ksm, vsm)

    feat_major = lambda c: jnp.transpose(c[0].reshape(bs, past, DIFF_WIDTH), (0, 2, 1))
    mlp_weights = (w_out_b, g_mlp, w_up_b, w_down_b, g_final2)
    samp_args = (qd_s, qs_s, kdn, ksn, vdn, vsn, kdm, ksm, vdm, vsm,
                 feat_major(cache_diff_k), feat_major(cache_sb_k),
                 cache_diff_v[0].reshape(bs, past * N_HEADS_DIFF, 2 * HEAD_DIM),
                 feat_major(cache_sb_v), *lams, g_diff_head)
    y_prompt, smix_d, smix_s = _mlp_and_samp_attn((x_prompt[0], mixed_d, mixed_s) + mlp_weights,
                                                  samp_args, ts)
    y_sample = _mlp(x_sample.reshape(bs * ts, d), smix_d, smix_s, *mlp_weights)

    def prompt_cache(a, tail):
        return a.reshape((1, bp, N_META + tp) + tail)

    def sample_cache(a, tail):
        return a.reshape((1, bs, ts) + tail)

    dk_tail, dv_tail = (N_HEADS_DIFF, 2, HEAD_DIM), (N_HEADS_DIFF, 2 * HEAD_DIM)
    sb_tail = (N_HEADS_SB, HEAD_DIM)
    return (y_prompt[None], y_sample.reshape(bs, ts, d),
            prompt_cache(kd, dk_tail), prompt_cache(vd, dv_tail),
            prompt_cache(ks, sb_tail), prompt_cache(vs, sb_tail),
            sample_cache(samp_f32[0], dk_tail), sample_cache(samp_f32[2], dv_tail),
            sample_cache(samp_f32[1], sb_tail), sample_cache(samp_f32[3], sb_tail))
```

```python
import functools
import math

import jax
import jax.numpy as jnp
from jax import lax
from jax.experimental import pallas as pl
from jax.experimental.pallas import tpu as pltpu

D_MODEL = 1024
CHUNK = 64
CHUNK_SHIFT = 6
N_META = 16
HEAD_DIM = 64
N_HEADS_DIFF = 4
N_HEADS_SB = 8
DIFF_WIDTH = N_HEADS_DIFF * 2 * HEAD_DIM
SB_WIDTH = N_HEADS_SB * HEAD_DIM
D_FF = 4 * D_MODEL
ROPE_THETA = 10000.0
NORM_EPS = 1e-6
NEG_INF = -1e30
LAMBDA_INIT = 0.8 - 0.6 * math.exp(-0.3 * 0)
LOG2_E = math.log2(math.e)

LANES = 128
META_PAD = LANES
SB_LOG2_STOP = -44.0
SCORE_LEAD = 3

VMEM_LIMIT = 56 * 1024 * 1024

_f32 = jnp.float32
_bf16 = jnp.bfloat16


def _dot(a, b):
    return jnp.dot(a, b, preferred_element_type=_f32)


def _dot_nt(a, b):
    return lax.dot_general(a, b, (((1,), (1,)), ((), ())), preferred_element_type=_f32)


def _rms(x, g):
    return x * lax.rsqrt(jnp.mean(x * x, axis=-1, keepdims=True) + NORM_EPS) * g


def _log2_sigmoid(s2):
    return jnp.minimum(s2, 0.0) - jnp.log2(1.0 + jnp.exp2(-jnp.abs(s2)))


def _first_half_lanes(shape):
    lane = lax.broadcasted_iota(jnp.int32, shape, 1)
    return (lane & (HEAD_DIM - 1)) < (HEAD_DIM // 2)


def _proj_math(x_ref, g_ref, w_ref, cos, sin):
    h = _rms(x_ref[...], g_ref[...]).astype(_bf16)
    first_half = _first_half_lanes(cos.shape)

    def rope(z):
        parts = []
        for j in range(z.shape[1] // LANES):
            zj = z[:, j * LANES:(j + 1) * LANES]
            swapped = jnp.where(first_half,
                                pltpu.roll(zj, LANES - HEAD_DIM // 2, 1),
                                pltpu.roll(zj, HEAD_DIM // 2, 1))
            parts.append(zj * cos + swapped * sin)
        return jnp.concatenate(parts, axis=1)

    def cols(c):
        return _dot(h, w_ref[:, c * DIFF_WIDTH:(c + 1) * DIFF_WIDTH])

    scale = HEAD_DIM ** -0.5
    scale = scale * LOG2_E
    return (rope(cols(0)) * scale, cols(1) * scale, rope(cols(2)), cols(3), cols(4), cols(5))


def _proj_kernel(x_ref, g_ref, w_ref, cos_ref, sin_ref, qd_ref, qs_ref, *kv_refs):
    qd, qs, kd, ks, vd, vs = _proj_math(x_ref, g_ref, w_ref, cos_ref[...], sin_ref[...])
    qd_ref[...] = qd[N_META:].astype(_bf16)
    qs_ref[...] = qs[N_META:].astype(_bf16)
    n = len(kv_refs) // 4
    meta_f32, samp_f32, meta_b16, samp_b16 = (kv_refs[i * n:(i + 1) * n] for i in range(4))
    for j, val in enumerate((kd, ks, vd, vs)):
        meta_f32[j][...] = val[:N_META]
        samp_f32[j][...] = val[N_META:]
        b16 = val.astype(_bf16)
        meta_b16[j][:N_META, :] = b16[:N_META]
        meta_b16[j][N_META:, :] = jnp.zeros((META_PAD - N_META, DIFF_WIDTH), _bf16)
        samp_b16[j][...] = b16[N_META:]


def _proj(x, g, w_bf16, cos, sin):
    rows = x.shape[0] - N_META
    full = lambda a: pl.BlockSpec(a.shape, lambda i: (0, 0))
    out = lambda r, dtype: jax.ShapeDtypeStruct((r, DIFF_WIDTH), dtype)
    out_shape = ([out(rows, _bf16)] * 2 + [out(N_META, _f32)] * 4 + [out(rows, _f32)] * 4
                 + [out(META_PAD, _bf16)] * 4 + [out(rows, _bf16)] * 4)
    res = pl.pallas_call(
        _proj_kernel,
        grid=(1,),
        in_specs=[full(x), full(g), full(w_bf16), full(cos), full(sin)],
        out_specs=[full(o) for o in out_shape],
        out_shape=out_shape,
        compiler_params=pltpu.CompilerParams(
            dimension_semantics=("arbitrary",), vmem_limit_bytes=VMEM_LIMIT),
        name="proj",
    )(x, g, w_bf16, cos, sin)
    return res[0], res[1], res[2:6], res[6:10], res[10:14], res[14:18]


def _proj_prompt_kernel(x_ref, g_ref, w_ref, cos_row_ref, sin_row_ref, cos_blk_ref, sin_blk_ref,
                        mkd_ref, mks_ref, mvd_ref, mvs_ref,
                        qd_ref, qs_ref, kd_ref, ks_ref, vd_ref, vs_ref,
                        kdb_ref, ksb_ref, vdb_ref, vsb_ref, tail_ref, *, tm):
    i = pl.program_id(0)
    last = pl.num_programs(0) - 1
    row_outs = ((0, kd_ref), (1, ks_ref), (3, vs_ref))

    def put_vd(token0, rows):
        for h in range(N_HEADS_DIFF):
            vd_ref[pl.ds(token0 * N_HEADS_DIFF + h, rows.shape[0], stride=N_HEADS_DIFF), :] = (
                rows[:, h * LANES:(h + 1) * LANES])

    @pl.when(i == 0)
    def _():
        for j, meta in enumerate((mkd_ref, mks_ref, mvd_ref, mvs_ref)):
            tail_ref[j] = meta[...]

    @pl.when(i < last)
    def _():
        cos_r, sin_r, cos_b, sin_b = cos_row_ref[...], sin_row_ref[...], cos_blk_ref[0], sin_blk_ref[0]
        cos = cos_b * cos_r - sin_b * sin_r
        sin = sin_b * cos_r + cos_b * sin_r
        sin = jnp.where(_first_half_lanes(sin.shape), -sin, sin)
        qd, qs, kd, ks, vd, vs = _proj_math(x_ref, g_ref, w_ref, cos, sin)
        qd_ref[...] = qd.astype(_bf16)
        qs_ref[...] = qs.astype(_bf16)
        for b16_ref, val in ((kdb_ref, kd), (ksb_ref, ks), (vdb_ref, vd), (vsb_ref, vs)):
            b16_ref[...] = val.astype(_bf16)
        for (j, out), val in zip(row_outs, (kd, ks, vs)):
            out[:N_META, :] = tail_ref[j]
            out[N_META:, :] = val[:tm - N_META]
            tail_ref[j] = val[tm - N_META:]
        put_vd(0, tail_ref[2])
        put_vd(N_META, vd[:tm - N_META])
        tail_ref[2] = vd[tm - N_META:]

    @pl.when(i == last)
    def _():
        for j, out in row_outs:
            out[:N_META, :] = tail_ref[j]
        put_vd(0, tail_ref[2])


def _proj_prompt(x, g, w_bf16, pos0, meta_kv, tm):
    rows = x.shape[0]
    nblk = rows // tm
    cos_row, sin_row = _angle_tables(jnp.arange(tm))
    cos_blk, sin_blk = [t[:, None, :] for t in _angle_tables(pos0 + tm * jnp.arange(nblk))]
    frame_blk = lambda width: pl.BlockSpec((tm, width), lambda i: (jnp.minimum(i, nblk - 1), 0))
    blk_angle = pl.BlockSpec((1, 1, LANES), lambda i: (jnp.minimum(i, nblk - 1), 0, 0))
    full = lambda a: pl.BlockSpec(a.shape, lambda i: (0, 0))
    f32_out = jax.ShapeDtypeStruct((N_META + rows, DIFF_WIDTH), _f32)
    vd_out = jax.ShapeDtypeStruct(((N_META + rows) * N_HEADS_DIFF, LANES), _f32)
    b16_out = jax.ShapeDtypeStruct((rows, DIFF_WIDTH), _bf16)
    stream_blk = pl.BlockSpec((tm, DIFF_WIDTH), lambda i: (i, 0))
    return pl.pallas_call(
        functools.partial(_proj_prompt_kernel, tm=tm),
        grid=(nblk + 1,),
        in_specs=[frame_blk(D_MODEL), full(g), full(w_bf16), full(cos_row), full(sin_row),
                  blk_angle, blk_angle] + [full(m) for m in meta_kv],
        out_specs=[frame_blk(DIFF_WIDTH)] * 2
                  + [stream_blk, stream_blk,
                     pl.BlockSpec((tm * N_HEADS_DIFF, LANES), lambda i: (i, 0)), stream_blk]
                  + [frame_blk(DIFF_WIDTH)] * 4,
        out_shape=[b16_out, b16_out, f32_out, f32_out, vd_out, f32_out,
                   b16_out, b16_out, b16_out, b16_out],
        scratch_shapes=[pltpu.VMEM((4, N_META, DIFF_WIDTH), _f32)],
        compiler_params=pltpu.CompilerParams(
            dimension_semantics=("arbitrary",), vmem_limit_bytes=VMEM_LIMIT),
        name="proj_prompt",
    )(x, g, w_bf16, cos_row, sin_row, cos_blk, sin_blk, *meta_kv)


def _angle_tables(pos):
    half = HEAD_DIM // 2
    inv_freq = ROPE_THETA ** (-jnp.arange(half, dtype=_f32) / half)
    ang = pos.astype(_f32)[:, None] * inv_freq[None, :]
    reps = LANES // half
    return jnp.tile(jnp.cos(ang), (1, reps)), jnp.tile(jnp.sin(ang), (1, reps))


def _rope_tables(pos):
    cos, sin = _angle_tables(pos)
    return cos, jnp.where(_first_half_lanes(sin.shape), -sin, sin)


def _lane_half_masks(shape):
    lane = lax.broadcasted_iota(jnp.int32, shape, 1)
    return lane < HEAD_DIM, lane >= HEAD_DIM


def _lambda(lq1, lk1, lq2, lk2):
    return (jnp.exp(jnp.sum(lq1 * lk1, axis=-1, keepdims=True))
            - jnp.exp(jnp.sum(lq2 * lk2, axis=-1, keepdims=True)) + LAMBDA_INIT)


def _diff_finish(o0, l0, o1, l1, lam, g_head):
    o = o0 / l0 - lam * (o1 / l1)
    return _rms(o, g_head) * (1.0 - LAMBDA_INIT)


def _diff_kernel(q_ref, k_ref, v_ref, km_ref, vm_ref, lq1_ref, lk1_ref, lq2_ref, lk2_ref, gh_ref,
                 o_ref, m_ref, acc_ref, *, bq, bk, heads):
    qi = pl.program_id(1)
    lo, hi = _lane_half_masks((bq, LANES))
    zero = jnp.zeros((bq, LANES), _bf16)
    chains = []
    for a in range(heads):
        sl = slice(a * LANES, (a + 1) * LANES)
        q = q_ref[:, sl]
        chains.append((2 * a, sl, jnp.where(lo, q, zero)))
        chains.append((2 * a + 1, sl, jnp.where(hi, q, zero)))

    def softmax_pv(c, rows, s, v, mask, first):
        if mask is not None:
            s = jnp.where(mask, s, NEG_INF)
        m_cur = jnp.max(s, axis=-1, keepdims=True)
        m_new = (jnp.broadcast_to(m_cur, (s.shape[0], LANES)) if first
                 else jnp.maximum(m_ref[c, rows], m_cur))
        p = jnp.exp2((s - jnp.tile(m_new, (1, s.shape[1] // LANES))).astype(_bf16))
        pv = _dot(p, jnp.concatenate([v, jnp.ones_like(v)], axis=1))
        if first:
            acc_ref[c, rows] = pv
        else:
            alpha = jnp.exp2(m_ref[c, rows] - m_new)
            acc_ref[c, rows] = jnp.tile(alpha, (1, 2)) * acc_ref[c, rows] + pv
        m_ref[c, rows] = m_new

    def step(items, first):
        scores = {}
        for i in range(len(items) + SCORE_LEAD):
            if i < len(items):
                scores[i] = _dot_nt(items[i][2], items[i][3]())
            if i >= SCORE_LEAD:
                c, rows, _, _, v_of, mask = items[i - SCORE_LEAD]
                softmax_pv(c, rows, scores.pop(i - SCORE_LEAD), v_of(), mask, first)

    q0 = qi * bq
    n_full = (q0 + CHUNK) // bk

    start = pl.multiple_of(n_full * bk, bk)
    hb = bq // 2
    first_items = []
    for h in range(2):
        rows, nkeys = slice(h * hb, (h + 1) * hb), (h + 1) * hb
        row = lax.broadcasted_iota(jnp.int32, (hb, nkeys + META_PAD), 0) + q0 + h * hb
        col = lax.broadcasted_iota(jnp.int32, (hb, nkeys + META_PAD), 1)
        key_chunk = jnp.where(col < nkeys, (col + start) >> CHUNK_SHIFT,
                              jnp.where(col < nkeys + N_META, -1, jnp.iinfo(jnp.int32).max))
        mask = key_chunk <= (row >> CHUNK_SHIFT)
        for c, sl, qz in chains:
            first_items.append((
                c, rows, qz[rows],
                lambda sl=sl, n=nkeys: jnp.concatenate([k_ref[pl.ds(start, n), sl], km_ref[:, sl]], axis=0),
                lambda sl=sl, n=nkeys: jnp.concatenate([v_ref[pl.ds(start, n), sl], vm_ref[:, sl]], axis=0),
                mask))
    step(first_items, True)

    def full_keys(start, width):
        start = pl.multiple_of(start, bk)
        step([(c, slice(None), qz, lambda sl=sl: k_ref[pl.ds(start, width), sl],
               lambda sl=sl: v_ref[pl.ds(start, width), sl], None) for c, sl, qz in chains], False)

    def tile_pair(j, carry):
        full_keys(j * 2 * bk, 2 * bk)
        return carry

    lax.fori_loop(0, n_full // 2, tile_pair, 0)

    @pl.when(n_full % 2 == 1)
    def _():
        full_keys((n_full - 1) * bk, bk)

    lam = _lambda(lq1_ref[...], lk1_ref[...], lq2_ref[...], lk2_ref[...])
    for a in range(heads):
        acc0, acc1 = acc_ref[2 * a], acc_ref[2 * a + 1]
        out = _diff_finish(acc0[:, :LANES], acc0[:, LANES:], acc1[:, :LANES], acc1[:, LANES:],
                           lam, gh_ref[...])
        o_ref[:, a * LANES:(a + 1) * LANES] = out.astype(_bf16)


def _diff_attn(qd, kdb, vdb, kmeta, vmeta, lq1, lk1, lq2, lk2, g_head, bq=512, bk=512,
               heads=N_HEADS_DIFF):
    n = qd.shape[0]
    assert bq == bk, "the split first step assumes the diagonal key tile starts at the query block"
    width = heads * LANES
    head_blk = lambda rows: pl.BlockSpec((rows, width), lambda h, i: (0, h),
                                         pipeline_mode=pl.Buffered(1))
    small = lambda a: pl.BlockSpec(a.shape, lambda h, i: (0, 0))
    return pl.pallas_call(
        functools.partial(_diff_kernel, bq=bq, bk=bk, heads=heads),
        grid=(N_HEADS_DIFF // heads, n // bq),
        in_specs=[pl.BlockSpec((bq, width), lambda h, i: (i, h)),
                  head_blk(n), head_blk(n), head_blk(META_PAD), head_blk(META_PAD),
                  small(lq1), small(lk1), small(lq2), small(lk2), small(g_head)],
        out_specs=pl.BlockSpec((bq, width), lambda h, i: (i, h)),
        out_shape=jax.ShapeDtypeStruct((n, DIFF_WIDTH), _bf16),
        scratch_shapes=[pltpu.VMEM((2 * heads, bq, LANES), _f32),
                        pltpu.VMEM((2 * heads, bq, 2 * LANES), _f32)],
        compiler_params=pltpu.CompilerParams(
            dimension_semantics=("arbitrary", "arbitrary"), vmem_limit_bytes=VMEM_LIMIT),
        name="diff_attn",
    )(qd, kdb, vdb, kmeta, vmeta, lq1, lk1, lq2, lk2, g_head)


def _suffix_matrix(n):
    j = lax.broadcasted_iota(jnp.int32, (n, n), 0)
    s = lax.broadcasted_iota(jnp.int32, (n, n), 1)
    return jnp.where(j > s, 1.0, 0.0).astype(_bf16)


def _sb_tile_phases(queries, keys, values, carries, mask, suffix, out):
    masks = mask if isinstance(mask, (list, tuple)) else [mask] * len(queries)
    scores = [_dot_nt(q, k) for q, k in zip(queries, keys)]
    yield
    log_betas, log_keeps = [], []
    for s, m in zip(scores, masks):
        log_beta = _log2_sigmoid(s)
        log_keep = log_beta - s
        if m is not None:
            log_keep = jnp.where(m, log_keep, 0.0)
        log_betas.append(log_beta)
        log_keeps.append(log_keep)
    afters = [_dot(log_keep.astype(_bf16), suffix) for log_keep in log_keeps]
    yield
    weights, new_carries = [], []
    for log_beta, log_keep, after, carry, m in zip(log_betas, log_keeps, afters, carries, masks):
        a = jnp.exp2(log_beta + after + carry)
        if m is not None:
            a = jnp.where(m, a, 0.0)
        weights.append(a.astype(_bf16))
        new_carries.append(carry + after[:, 0:1] + log_keep[:, 0:1])
    out.append(([_dot(a, v) for a, v in zip(weights, values)], new_carries))


def _sb_kernel(q_ref, k_ref, v_ref, km_ref, vm_ref, o_ref, acc_ref, carry_ref, live_ref, *, bq):
    qi = pl.program_id(0)
    lo, hi = _lane_half_masks((bq, LANES))
    zero = jnp.zeros((bq, LANES), _bf16)
    pairs = [slice(p * LANES, (p + 1) * LANES) for p in range(SB_WIDTH // LANES)]
    lanes = [sl for sl in pairs for _ in range(2)]
    heads = range(N_HEADS_SB)
    bk = 2 * bq
    queries = []
    for sl in pairs:
        q = q_ref[:, sl]
        queries += [jnp.where(lo, q, zero), jnp.where(hi, q, zero)]

    def older_tile(keys, values, mask, width):
        result = []
        _run_phases(_sb_tile_phases(queries, keys, values, [carry_ref[c] for c in heads], mask,
                                    _suffix_matrix(width), result))
        pvs, carries = result[0]
        live = carries[0]
        for c in heads:
            acc_ref[c] += pvs[c]
            carry_ref[c] = carries[c]
            live = jnp.maximum(live, carries[c])
        return jnp.max(live)

    def frame_tile(start, width):
        start = pl.multiple_of(start, bq)
        return older_tile([k_ref[pl.ds(start, width), sl] for sl in lanes],
                          [v_ref[pl.ds(start, width), sl] for sl in lanes], None, width)

    def half_tiles(starts, masks, first):
        qs2, ks2, vs2, carries = [], [], [], []
        for h, rows in enumerate((slice(0, half), slice(half, bq))):
            start = pl.multiple_of(starts[h], half)
            for c in heads:
                qs2.append(queries[c][rows])
                ks2.append(k_ref[pl.ds(start, bq), lanes[c]])
                vs2.append(v_ref[pl.ds(start, bq), lanes[c]])
                carries.append(jnp.zeros((half, 1), _f32) if first else carry_ref[c, rows])
        result = []
        _run_phases(_sb_tile_phases(qs2, ks2, vs2, carries, [m for m in masks for _ in heads],
                                    _suffix_matrix(bq), result))
        pvs, carries = result[0]
        live = carries[0]
        for h, rows in enumerate((slice(0, half), slice(half, bq))):
            for c in heads:
                i = h * N_HEADS_SB + c
                acc_ref[c, rows] = pvs[i] if first else acc_ref[c, rows] + pvs[i]
                carry_ref[c, rows] = carries[i]
                live = jnp.maximum(live, carries[i])
        return jnp.max(live)

    q0 = qi * bq
    half = bq // 2
    start_a = jnp.maximum(q0 - half, 0)
    row = lax.broadcasted_iota(jnp.int32, (half, bq), 0) + q0
    col = lax.broadcasted_iota(jnp.int32, (half, bq), 1)
    live_ref[0] = half_tiles((start_a, q0), (col + start_a < row, col + q0 < row + half), True)

    @pl.when(jnp.logical_and(q0 > 0, live_ref[0] > SB_LOG2_STOP))
    def _():
        live_ref[0] = half_tiles((q0 - bq, q0 - bq), (col < half, None), False)

    start0 = jnp.maximum(q0 - bq, 0)

    def cond(state):
        start, live = state
        return jnp.logical_and(start >= bk, live > SB_LOG2_STOP)

    def body(state):
        start, _ = state
        return start - bk, frame_tile(start - bk, bk)

    start, live = lax.while_loop(cond, body, (start0, live_ref[0]))
    live_ref[0] = live

    @pl.when(jnp.logical_and(start == bq, live > SB_LOG2_STOP))
    def _():
        live_ref[0] = frame_tile(0, bq)

    @pl.when(jnp.logical_and(start < bk, live_ref[0] > SB_LOG2_STOP))
    def _():
        mcol = lax.broadcasted_iota(jnp.int32, (bq, META_PAD), 1)
        older_tile([km_ref[:, sl] for sl in lanes], [vm_ref[:, sl] for sl in lanes], mcol < N_META,
                   META_PAD)

    for p, sl in enumerate(pairs):
        o_ref[:, sl] = jnp.where(lo, acc_ref[2 * p], acc_ref[2 * p + 1]).astype(_bf16)


def _sb_attn(qs, ksb, vsb, kmeta, vmeta, bq=128):
    n = qs.shape[0]
    resident = lambda a: pl.BlockSpec(a.shape, lambda i: (0, 0), pipeline_mode=pl.Buffered(1))
    return pl.pallas_call(
        functools.partial(_sb_kernel, bq=bq),
        grid=(n // bq,),
        in_specs=[pl.BlockSpec((bq, SB_WIDTH), lambda i: (i, 0)),
                  resident(ksb), resident(vsb), resident(kmeta), resident(vmeta)],
        out_specs=pl.BlockSpec((bq, SB_WIDTH), lambda i: (i, 0)),
        out_shape=jax.ShapeDtypeStruct((n, SB_WIDTH), _bf16),
        scratch_shapes=[pltpu.VMEM((N_HEADS_SB, bq, LANES), _f32),
                        pltpu.VMEM((N_HEADS_SB, bq, 1), _f32),
                        pltpu.SMEM((1,), _f32)],
        compiler_params=pltpu.CompilerParams(
            dimension_semantics=("arbitrary",), vmem_limit_bytes=VMEM_LIMIT),
        name="sb_attn",
    )(qs, ksb, vsb, kmeta, vmeta)


def _run_phases(*phase_generators):
    pending = list(phase_generators)
    while pending:
        pending = [g for g in pending if next(g, "done") != "done"]


def _samp_phases(qd_ref, qs_ref, kdn_ref, ksn_ref, vdn_ref, vsn_ref,
                 kdm_ref, ksm_ref, vdm_ref, vsm_ref,
                 ckd_ref, cks_ref, cvd_ref, cvs_ref,
                 lq1_ref, lk1_ref, lq2_ref, lk2_ref, gh_ref,
                 od_ref, os_ref, kd_edge, ks_edge, vd_edge, vs_edge, *, ts, past):
    for edge, meta, new in ((kd_edge, kdm_ref, kdn_ref), (ks_edge, ksm_ref, ksn_ref),
                            (vd_edge, vdm_ref, vdn_ref), (vs_edge, vsm_ref, vsn_ref)):
        edge[0:N_META, :] = meta[0:N_META, :]
        edge[N_META:N_META + ts, :] = new[...]
        edge[N_META + ts:, :] = jnp.zeros((LANES - N_META - ts, DIFF_WIDTH), _bf16)

    grp = 2 * ts
    n_cols = past + LANES
    col = lax.broadcasted_iota(jnp.int32, (grp, n_cols), 1)
    ecol = col - past
    pos_k = jnp.where(ecol < 0, col + N_META, jnp.where(ecol < N_META, ecol, ecol + past))
    pos_q = (lax.broadcasted_iota(jnp.int32, (grp, n_cols), 0) & (ts - 1)) + N_META + past
    valid = ecol < N_META + ts
    chunk = lambda p: jnp.where(p < N_META, -1, (p - N_META) >> CHUNK_SHIFT)
    mask_chunk = jnp.logical_and(valid, chunk(pos_k) <= chunk(pos_q))
    mask_strict = jnp.logical_and(valid, pos_k < pos_q)

    lo, hi = _lane_half_masks((ts, LANES))
    zero = jnp.zeros((ts, LANES), _bf16)
    blocks = [slice(j * LANES, (j + 1) * LANES) for j in range(DIFF_WIDTH // LANES)]

    def stacked(q):
        return jnp.concatenate([jnp.where(lo, q, zero), jnp.where(hi, q, zero)], axis=0)

    def scores(q_ref, cache_ref, edge):
        out = []
        for sl in blocks:
            q = stacked(q_ref[:, sl])
            out.append(jnp.concatenate([_dot(q, cache_ref[0, sl, :].astype(_bf16)),
                                        _dot_nt(q, edge[:, sl])], axis=1))
        return out

    diff_scores = scores(qd_ref, ckd_ref, kd_edge)
    sb_scores = scores(qs_ref, cks_ref, ks_edge)
    yield

    lam = _lambda(lq1_ref[...], lk1_ref[...], lq2_ref[...], lk2_ref[...])
    probs, sums = [], []
    for s in diff_scores:
        s = jnp.where(mask_chunk, s, NEG_INF)
        p = jnp.exp2(s - jnp.max(s, axis=-1, keepdims=True))
        probs.append(p.astype(_bf16))
        sums.append(jnp.sum(p, axis=-1, keepdims=True))
    for h, (sl, p, l) in enumerate(zip(blocks, probs, sums)):
        v_cache = cvd_ref[0, pl.ds(h, past, stride=N_HEADS_DIFF), :].astype(_bf16)
        pv = _dot(p[:, :past], v_cache) + _dot(p[:, past:], vd_edge[:, sl])
        out = _diff_finish(pv[:ts], l[:ts], pv[ts:], l[ts:], lam, gh_ref[...])
        od_ref[:, sl] = out.astype(_bf16)
    yield

    n_tiles = past // LANES
    tiles = [slice(t * LANES, (t + 1) * LANES) for t in range(n_tiles)]
    edge_cols = slice(past, n_cols)
    log_betas, log_keeps, cache_pieces, edge_pieces = [], [], [], []
    for s in sb_scores:
        log_beta = _log2_sigmoid(s)
        log_keep = jnp.where(mask_strict, log_beta - s, 0.0)
        keep_hi = log_keep.astype(_bf16)
        keep_lo = (log_keep - keep_hi.astype(_f32)).astype(_bf16)
        log_betas.append(log_beta)
        log_keeps.append(log_keep)
        cache_pieces += [keep_hi[:, t] for t in tiles] + [keep_lo[:, t] for t in tiles]
        edge_pieces += [keep_hi[:, edge_cols], keep_lo[:, edge_cols]]
    after_cache = _dot(jnp.concatenate(cache_pieces, axis=0), _suffix_matrix(LANES))
    ej = lax.broadcasted_iota(jnp.int32, (LANES, LANES), 0)
    es = lax.broadcasted_iota(jnp.int32, (LANES, LANES), 1)
    edge_suffix = jnp.where(jnp.logical_and(ej > es, (ej < N_META) == (es < N_META)), 1.0, 0.0)
    after_edge = _dot(jnp.concatenate(edge_pieces, axis=0), edge_suffix.astype(_bf16))
    yield
    is_meta = lax.broadcasted_iota(jnp.int32, (grp, LANES), 1) < N_META
    for j, sl in enumerate(blocks):
        base = j * 2 * n_tiles
        after = [after_cache[(base + t) * grp:(base + t + 1) * grp]
                 + after_cache[(base + n_tiles + t) * grp:(base + n_tiles + t + 1) * grp]
                 for t in range(n_tiles)]
        edge_after = (after_edge[2 * j * grp:(2 * j + 1) * grp]
                      + after_edge[(2 * j + 1) * grp:(2 * j + 2) * grp])
        keep_edge = log_keeps[j][:, edge_cols]
        carry = edge_after[:, N_META:N_META + 1] + keep_edge[:, N_META:N_META + 1]
        shifted = [None] * n_tiles
        for t in reversed(range(n_tiles)):
            shifted[t] = after[t] + carry
            carry = carry + after[t][:, 0:1] + log_keeps[j][:, tiles[t]][:, 0:1]
        shifted.append(edge_after + jnp.where(is_meta, carry, 0.0))
        a = jnp.exp2(log_betas[j] + jnp.concatenate(shifted, axis=1))
        a = jnp.where(mask_strict, a, 0.0).astype(_bf16)
        pv = _dot_nt(a[:, :past], cvs_ref[0, sl, :].astype(_bf16)) + _dot(a[:, past:], vs_edge[:, sl])
        os_ref[:, sl] = jnp.where(lo, pv[:ts], pv[ts:]).astype(_bf16)


N_SAMP_INPUTS = 19


def _samp_call_parts(args, ts):
    assert len(args) == N_SAMP_INPUTS
    nb, _, past = args[10].shape
    assert past % LANES == 0 and N_META + ts <= LANES
    row_blk = pl.BlockSpec((ts, DIFF_WIDTH), lambda b: (b, 0))
    per_batch = lambda a: pl.BlockSpec((1,) + a.shape[1:], lambda b: (b, 0, 0))
    small = lambda a: pl.BlockSpec(a.shape, lambda b: (0, 0))
    out = jax.ShapeDtypeStruct((nb * ts, DIFF_WIDTH), _bf16)
    in_specs = ([row_blk] * 6 + [small(a) for a in args[6:10]] + [per_batch(a) for a in args[10:14]]
                + [small(a) for a in args[14:]])
    scratch = [pltpu.VMEM((LANES, DIFF_WIDTH), _bf16)] * 4
    return dict(ts=ts, past=past), nb, in_specs, [row_blk, row_blk], [out, out], scratch


def _mlp_kernel(*refs, tf):
    _run_phases(_mlp_phases(*refs, tf=tf))


def _mlp_phases(x_ref, md_ref, ms_ref, wo_ref, gm_ref, wu_ref, wd_ref, gf_ref, o_ref, h2_ref, *, tf):
    x1 = (x_ref[...] + _dot(md_ref[...], wo_ref[:DIFF_WIDTH, :])
          + _dot(ms_ref[...], wo_ref[DIFF_WIDTH:, :]))
    h2_ref[...] = _rms(x1, gm_ref[...]).astype(_bf16)
    o_ref[...] = x1
    yield
    for f in range(D_FF // tf):
        u = jnp.maximum(_dot(h2_ref[...], wu_ref[:, f * tf:(f + 1) * tf]), 0.0)
        o_ref[...] += _dot((u * u).astype(_bf16), wd_ref[f * tf:(f + 1) * tf, :])
        yield
    o_ref[...] = _rms(o_ref[...], gf_ref[...])


N_MLP_INPUTS = 8
MLP_ROWS = 512
MLP_FF_CHUNK = 1024


def _mlp_call_parts(args):
    assert len(args) == N_MLP_INPUTS
    x, _, _, wo, g_mlp, wu, wd, g_final = args
    rows, tm = x.shape[0], MLP_ROWS
    resident = lambda a: pl.BlockSpec(a.shape, lambda i: (0, 0), pipeline_mode=pl.Buffered(1))
    small = lambda a: pl.BlockSpec(a.shape, lambda i: (0, 0))
    row_blk = lambda width: pl.BlockSpec((tm, width), lambda i: (i, 0))
    in_specs = [row_blk(D_MODEL), row_blk(DIFF_WIDTH), row_blk(SB_WIDTH),
                resident(wo), small(g_mlp), resident(wu), resident(wd), small(g_final)]
    return (rows // tm, in_specs, row_blk(D_MODEL), jax.ShapeDtypeStruct((rows, D_MODEL), _f32),
            [pltpu.VMEM((tm, D_MODEL), _bf16)])


def _mlp(*args):
    steps, in_specs, out_spec, out_shape, scratch = _mlp_call_parts(args)
    return pl.pallas_call(
        functools.partial(_mlp_kernel, tf=MLP_FF_CHUNK),
        grid=(steps,),
        in_specs=in_specs, out_specs=out_spec, out_shape=out_shape, scratch_shapes=scratch,
        compiler_params=pltpu.CompilerParams(
            dimension_semantics=("arbitrary",), vmem_limit_bytes=VMEM_LIMIT),
        name="mlp",
    )(*args)


def _mlp_samp_kernel(*refs, samp_kwargs):
    n_in = N_MLP_INPUTS + N_SAMP_INPUTS
    mlp_in, samp_in = refs[:N_MLP_INPUTS], refs[N_MLP_INPUTS:n_in]
    y_ref, od_ref, os_ref, h2_ref = refs[n_in:n_in + 4]
    edges = refs[n_in + 4:]
    _run_phases(_samp_phases(*samp_in, od_ref, os_ref, *edges, **samp_kwargs),
                _mlp_phases(*mlp_in, y_ref, h2_ref, tf=MLP_FF_CHUNK))


def _mlp_and_samp_attn(mlp_args, samp_args, ts):
    steps, m_in, m_out, m_shape, m_scratch = _mlp_call_parts(mlp_args)
    kwargs, nb, s_in, s_out, s_shape, s_scratch = _samp_call_parts(samp_args, ts)
    assert steps == nb, "prompt MLP row blocks and decode batch elements must pair up one to one"
    return pl.pallas_call(
        functools.partial(_mlp_samp_kernel, samp_kwargs=kwargs),
        grid=(steps,),
        in_specs=m_in + s_in, out_specs=[m_out] + s_out, out_shape=[m_shape] + s_shape,
        scratch_shapes=m_scratch + s_scratch,
        compiler_params=pltpu.CompilerParams(
            dimension_semantics=("arbitrary",), vmem_limit_bytes=VMEM_LIMIT),
        name="mlp_samp_attn",
    )(*mlp_args, *samp_args)


def kernel(x_prompt, x_sample, cache_diff_k, cache_diff_v, cache_sb_k, cache_sb_v, meta_tokens, g_mix, w_in, lambda_q1, lambda_k1, lambda_q2, lambda_k2, g_diff_head, w_out, g_mlp, w_up, w_down, g_final):
    bp, tp, d = x_prompt.shape
    bs, ts, _ = x_sample.shape
    past = cache_diff_k.shape[2]
    assert bp == 1 and g_mix.shape[0] == 1, "single prompt sequence, depth 1"

    w_in_b = w_in[0].astype(_bf16)
    w_out_b = w_out[0].astype(_bf16)
    w_up_b = w_up[0].astype(_bf16)
    w_down_b = w_down[0].astype(_bf16)
    g_final2 = g_final[None, :]
    lams = (lambda_q1, lambda_k1, lambda_q2, lambda_k2)

    x_ms = jnp.concatenate([meta_tokens.astype(x_sample.dtype), x_sample.reshape(bs * ts, d)], axis=0)
    pos_ms = jnp.concatenate([jnp.arange(N_META), jnp.tile(N_META + past + jnp.arange(ts), bs)])
    cos_s, sin_s = _rope_tables(pos_ms)
    qd_s, qs_s, meta_f32, samp_f32, (kdm, ksm, vdm, vsm), (kdn, ksn, vdn, vsn) = _proj(
        x_ms, g_mix, w_in_b, cos_s, sin_s)

    p_out = _proj_prompt(x_prompt[0], g_mix, w_in_b, N_META, meta_f32, tm=512)
    qd, qs, kd, ks, vd, vs, kdb, ksb, vdb, vsb = p_out

    mixed_d = _diff_attn(qd, kdb, vdb, kdm, vdm, *lams, g_diff_head)
    mixed_s = _sb_attn(qs, ksb, vsb, ksm, vsm)

    feat_major = lambda c: jnp.transpose(c[0].reshape(bs, past, DIFF_WIDTH), (0, 2, 1))
    mlp_weights = (w_out_b, g_mlp, w_up_b, w_down_b, g_final2)
    samp_args = (qd_s, qs_s, kdn, ksn, vdn, vsn, kdm, ksm, vdm, vsm,
                 feat_major(cache_diff_k), feat_major(cache_sb_k),
                 cache_diff_v[0].reshape(bs, past * N_HEADS_DIFF, 2 * HEAD_DIM),
                 feat_major(cache_sb_v), *lams, g_diff_head)
    y_prompt, smix_d, smix_s = _mlp_and_samp_attn((x_prompt[0], mixed_d, mixed_s) + mlp_weights,
                                                  samp_args, ts)
    y_sample = _mlp(x_sample.reshape(bs * ts, d), smix_d, smix_s, *mlp_weights)

    def prompt_cache(a, tail):
        return a.reshape((1, bp, N_META + tp) + tail)

    def sample_cache(a, tail):
        return a.reshape((1, bs, ts) + tail)

    dk_tail, dv_tail = (N_HEADS_DIFF, 2, HEAD_DIM), (N_HEADS_DIFF, 2 * HEAD_DIM)
    sb_tail = (N_HEADS_SB, HEAD_DIM)
    return (y_prompt[None], y_sample.reshape(bs, ts, d),
            prompt_cache(kd, dk_tail), prompt_cache(vd, dv_tail),
            prompt_cache(ks, sb_tail), prompt_cache(vs, sb_tail),
            sample_cache(samp_f32[0], dk_tail), sample_cache(samp_f32[2], dv_tail),
            sample_cache(samp_f32[1], sb_tail), sample_cache(samp_f32[3], sb_tail))
```
